```python
import jax, jax.numpy as jnp
from jax import lax
import numpy as np

D_MODEL = 1024
BATCH = 16
SEQ = 2048
DEPTH = 4

N_BRANCH = 3
MIX_W = D_MODEL // 2

RWKV_HEAD = 64
RWKV_HEADS = MIX_W // RWKV_HEAD
RWKV_DECAY_RANK = 64
RWKV_A_RANK = 64
RWKV_G_RANK = 128
RWKV_GN_EPS = RWKV_HEAD * 1e-5

S5_GROUP = 16
S5_GROUPS = MIX_W // S5_GROUP
S5_STATE = 64

GLA_HEADS = 4
GLA_DV = MIX_W // GLA_HEADS
GLA_DK = GLA_DV // 2
GLA_GATE_RANK = 16
GLA_GATE_NORM = 16.0
GLA_CHUNK = 64

N_MEM = 256
XA_HEADS = 4
XA_HEAD = D_MODEL // XA_HEADS

D_FF = 2816
CONV_W = 3
NORM_EPS = 1e-6

RWKV_SIZES = (MIX_W, MIX_W, MIX_W, RWKV_DECAY_RANK, RWKV_A_RANK, RWKV_G_RANK)
RWKV_COLS = 3 * MIX_W + RWKV_DECAY_RANK + RWKV_A_RANK + RWKV_G_RANK
S5_COLS = MIX_W
GLA_SIZES = (GLA_HEADS * GLA_DK, GLA_HEADS * GLA_DK, MIX_W, GLA_GATE_RANK, MIX_W)
GLA_COLS = 2 * GLA_HEADS * GLA_DK + 2 * MIX_W + GLA_GATE_RANK
GATE_COLS = N_BRANCH * D_MODEL
IN_SIZES = (RWKV_COLS, S5_COLS, GLA_COLS, GATE_COLS)
N_IN = RWKV_COLS + S5_COLS + GLA_COLS + GATE_COLS

kernel_name = "hybrid_rwkv7_s5_gla_gated_trunk"

F32 = jnp.float32


def _split(t, sizes):
    idx = np.cumsum(np.array(sizes))[:-1].tolist()
    return jnp.split(t, idx, axis=-1)


def rms_norm(x, g):
    xf = x.astype(F32)
    y = xf * lax.rsqrt(jnp.mean(xf * xf, axis=-1, keepdims=True) + NORM_EPS)
    return (y * g).astype(x.dtype)


def _token_shift(p):
    return jnp.pad(p, ((0, 0), (1, 0), (0, 0)))[:, :-1]


def rwkv7_mix(p, mu, w0, w_up, a0, a_up, g_up, k_k, k_a, r_k, ln_w, ln_b):
    B, L, _ = p.shape
    p = p + (_token_shift(p) - p) * mu
    r, k, v, wd, ad, gd = _split(p, RWKV_SIZES)
    w_log = -jax.nn.softplus(-(w0 + jnp.tanh(wd) @ w_up)) - 0.5
    a = jax.nn.sigmoid(a0 + ad @ a_up)
    g = jax.nn.sigmoid(gd) @ g_up

    def heads(t):
        return t.reshape(B, L, RWKV_HEADS, RWKV_HEAD).astype(F32)

    kk = heads(k * k_k)
    kk = kk * lax.rsqrt(jnp.sum(kk * kk, axis=-1, keepdims=True) + 1e-12)
    k = k * (1 + (a - 1) * k_a)
    r_h, k_h, v_h, a_h = heads(r), heads(k), heads(v), heads(a)
    decay = jnp.exp(-jnp.exp(heads(w_log)))

    def step(S, inp):
        r_t, w_t, k_t, v_t, kk_t, a_t = inp
        sa = jnp.einsum('bhvk,bhk->bhv', S, -kk_t)
        S = (S * w_t[:, :, None, :]
             + sa[..., None] * (kk_t * a_t)[:, :, None, :]
             + v_t[..., None] * k_t[:, :, None, :])
        return S, jnp.einsum('bhvk,bhk->bhv', S, r_t)

    seq = tuple(jnp.moveaxis(t, 1, 0) for t in (r_h, decay, k_h, v_h, kk, a_h))
    S0 = jnp.zeros((B, RWKV_HEADS, RWKV_HEAD, RWKV_HEAD), F32)
    _, y = lax.scan(step, S0, seq)
    y = jnp.moveaxis(y, 0, 1)
    mean = jnp.mean(y, axis=-1, keepdims=True)
    var = jnp.mean(jnp.square(y - mean), axis=-1, keepdims=True)
    y = ((y - mean) * lax.rsqrt(var + RWKV_GN_EPS)).reshape(B, L, MIX_W) * ln_w + ln_b
    bonus = jnp.sum(r_h * k_h * r_k, axis=-1, keepdims=True) * v_h
    y = (y + bonus.reshape(B, L, MIX_W)) * g
    return y.astype(p.dtype)


def _complex_affine_combine(e1, e2):
    a1r, a1i, b1r, b1i = e1
    a2r, a2i, b2r, b2i = e2
    return (a2r * a1r - a2i * a1i,
            a2r * a1i + a2i * a1r,
            a2r * b1r - a2i * b1i + b2r,
            a2r * b1i + a2i * b1r + b2i)


def s5_mix(u, a_re, a_im, b_re, b_im, c_re, c_im, d, log_step, w_glu, b_glu):
    B, L, _ = u.shape
    uf = u.astype(F32).reshape(B, L, S5_GROUPS, S5_GROUP)
    ar, ai = a_re.astype(F32), a_im.astype(F32)
    dt = jnp.exp(log_step.astype(F32))[:, None]
    mag = jnp.exp(dt * ar)
    ang = dt * ai
    abar_re, abar_im = mag * jnp.cos(ang), mag * jnp.sin(ang)
    den = ar * ar + ai * ai
    nr, ni = abar_re - 1.0, abar_im
    f_re = ((nr * ar + ni * ai) / den)[..., None]
    f_im = ((ni * ar - nr * ai) / den)[..., None]
    br, bi = b_re.astype(F32), b_im.astype(F32)
    bbar_re = f_re * br - f_im * bi
    bbar_im = f_re * bi + f_im * br
    bu_re = jnp.einsum('blgc,gnc->blgn', uf, bbar_re)
    bu_im = jnp.einsum('blgc,gnc->blgn', uf, bbar_im)
    shape = (1, L, S5_GROUPS, S5_STATE)
    elems = (jnp.broadcast_to(abar_re, shape), jnp.broadcast_to(abar_im, shape), bu_re, bu_im)
    _, _, s_re, s_im = lax.associative_scan(_complex_affine_combine, elems, axis=1)
    y = (jnp.einsum('blgn,gcn->blgc', s_re, c_re.astype(F32))
         - jnp.einsum('blgn,gcn->blgc', s_im, c_im.astype(F32))
         + d.astype(F32) * uf)
    y = jax.nn.gelu(y.reshape(B, L, MIX_W))
    y = y * jax.nn.sigmoid(y @ w_glu.astype(F32) + b_glu.astype(F32))
    return y.astype(u.dtype)


def gla_mix(p, gk_up, gk_b, norm_g):
    B, L, _ = p.shape
    nc = L // GLA_CHUNK
    q, k, v, gkd, go = _split(p, GLA_SIZES)
    gk = jax.nn.log_sigmoid((gkd @ gk_up + gk_b).astype(F32)) / GLA_GATE_NORM

    def chunks(t, dh):
        return t.astype(F32).reshape(B, nc, GLA_CHUNK, GLA_HEADS, dh).transpose(0, 3, 1, 2, 4)

    q = chunks(q, GLA_DK) * (GLA_DK ** -0.5)
    k = chunks(k, GLA_DK)
    v = chunks(v, GLA_DV)
    b = jnp.cumsum(chunks(gk, GLA_DK), axis=3)
    b_last = b[:, :, :, -1:, :]
    q_dec = q * jnp.exp(b)
    k_inv = k * jnp.exp(-b)
    mask = jnp.tril(jnp.ones((GLA_CHUNK, GLA_CHUNK), dtype=bool))
    att = jnp.where(mask, jnp.einsum('bhncd,bhnsd->bhncs', q_dec, k_inv), 0.0)
    o = jnp.einsum('bhncs,bhnse->bhnce', att, v)
    u = jnp.einsum('bhncd,bhnce->bhnde', k * jnp.exp(b_last - b), v)

    def step(S, inp):
        u_c, dec_c = inp
        return jnp.exp(dec_c)[..., None] * S + u_c, S

    S0 = jnp.zeros((B, GLA_HEADS, GLA_DK, GLA_DV), F32)
    _, S_prev = lax.scan(step, S0, (jnp.moveaxis(u, 2, 0), jnp.moveaxis(b_last[:, :, :, 0], 2, 0)))
    S_prev = jnp.moveaxis(S_prev, 0, 2)
    o = o + jnp.einsum('bhncd,bhnde->bhnce', q_dec, S_prev)
    o = o.transpose(0, 2, 3, 1, 4).reshape(B, L, GLA_HEADS, GLA_DV)
    o = o * lax.rsqrt(jnp.mean(o * o, axis=-1, keepdims=True) + 1e-5) * norm_g
    o = o.reshape(B, L, MIX_W) * jax.nn.silu(go.astype(F32))
    return o.astype(p.dtype)


def cross_attention(h, mem_n, wq, wkv, wo):
    B, L, _ = h.shape
    M = mem_n.shape[1]
    q = (h @ wq).reshape(B, L, XA_HEADS, XA_HEAD)
    kv = (mem_n @ wkv).reshape(B, M, 2, XA_HEADS, XA_HEAD)
    k, v = kv[:, :, 0], kv[:, :, 1]
    s = jnp.einsum('bqhd,bkhd->bhqk', q, k).astype(F32) * (XA_HEAD ** -0.5)
    pr = jax.nn.softmax(s, axis=-1).astype(h.dtype)
    o = jnp.einsum('bhqk,bkhd->bqhd', pr, v).reshape(B, L, D_MODEL)
    return o @ wo


def conv_ffn(h, w_up, conv_w, conv_b, w_down):
    u = h @ w_up
    u = lax.conv_general_dilated(
        u, conv_w[:, None, :], window_strides=(1,), padding=[(CONV_W - 1, 0)],
        dimension_numbers=('NWC', 'WIO', 'NWC'), feature_group_count=2 * D_FF) + conv_b
    gate, val = jnp.split(u, 2, axis=-1)
    return (jax.nn.silu(gate) * val) @ w_down


def setup_inputs(seed: int = 0) -> dict:
    key = jax.random.key(seed)
    keys = iter(jax.random.split(key, 64))

    def nrm(shape, scale):
        return jax.random.normal(next(keys), shape, F32) * scale

    def unif(shape, lo, hi):
        return jax.random.uniform(next(keys), shape, F32, lo, hi)

    def gain(shape):
        return 1.0 + nrm(shape, 0.02)

    Ly = DEPTH
    n_idx = jnp.arange(S5_STATE, dtype=F32)
    return {
        "x": nrm((BATCH, SEQ, D_MODEL), 1.0),
        "mem": nrm((BATCH, N_MEM, D_MODEL), 1.0),
        "norm_mix": gain((Ly, D_MODEL)),
        "w_in": nrm((Ly, D_MODEL, N_IN), D_MODEL ** -0.5),
        "rw_mu": unif((Ly, RWKV_COLS), 0.0, 1.0),
        "rw_w0": unif((Ly, MIX_W), -6.0, -1.0),
        "rw_w_up": nrm((Ly, RWKV_DECAY_RANK, MIX_W), RWKV_DECAY_RANK ** -0.5),
        "rw_a0": nrm((Ly, MIX_W), 0.1),
        "rw_a_up": nrm((Ly, RWKV_A_RANK, MIX_W), RWKV_A_RANK ** -0.5),
        "rw_g_up": nrm((Ly, RWKV_G_RANK, MIX_W), RWKV_G_RANK ** -0.5),
        "rw_k_k": 0.85 + nrm((Ly, MIX_W), 0.02),
        "rw_k_a": 1.0 + nrm((Ly, MIX_W), 0.02),
        "rw_r_k": nrm((Ly, RWKV_HEADS, RWKV_HEAD), 0.1),
        "rw_ln_w": gain((Ly, MIX_W)),
        "rw_ln_b": nrm((Ly, MIX_W), 0.02),
        "s5_a_re": -0.5 * jnp.exp(nrm((Ly, S5_GROUPS, S5_STATE), 0.02)),
        "s5_a_im": jnp.pi * n_idx + nrm((Ly, S5_GROUPS, S5_STATE), 0.02),
        "s5_b_re": nrm((Ly, S5_GROUPS, S5_STATE, S5_GROUP), (2 * S5_GROUP) ** -0.5),
        "s5_b_im": nrm((Ly, S5_GROUPS, S5_STATE, S5_GROUP), (2 * S5_GROUP) ** -0.5),
        "s5_c_re": nrm((Ly, S5_GROUPS, S5_GROUP, S5_STATE), (2 * S5_STATE) ** -0.5),
        "s5_c_im": nrm((Ly, S5_GROUPS, S5_GROUP, S5_STATE), (2 * S5_STATE) ** -0.5),
        "s5_d": nrm((Ly, S5_GROUPS, S5_GROUP), 1.0),
        "s5_log_step": unif((Ly, S5_GROUPS), float(np.log(1e-3)), float(np.log(1e-1))),
        "s5_w_glu": nrm((Ly, MIX_W, MIX_W), MIX_W ** -0.5),
        "s5_b_glu": nrm((Ly, MIX_W), 0.02),
        "gla_gk_up": nrm((Ly, GLA_GATE_RANK, GLA_HEADS * GLA_DK), GLA_GATE_RANK ** -0.5),
        "gla_gk_b": unif((Ly, GLA_HEADS * GLA_DK), 0.0, 3.0),
        "gla_norm": gain((Ly, GLA_DV)),
        "w_branch": nrm((Ly, N_BRANCH, MIX_W, D_MODEL), MIX_W ** -0.5),
        "w_out": nrm((Ly, D_MODEL, D_MODEL), D_MODEL ** -0.5),
        "norm_mem": gain((D_MODEL,)),
        "norm_xattn": gain((Ly, D_MODEL)),
        "xa_wq": nrm((Ly, D_MODEL, D_MODEL), D_MODEL ** -0.5),
        "xa_wkv": nrm((Ly, D_MODEL, 2 * D_MODEL), D_MODEL ** -0.5),
        "xa_wo": nrm((Ly, D_MODEL, D_MODEL), D_MODEL ** -0.5),
        "norm_ffn": gain((Ly, D_MODEL)),
        "ffn_w_up": nrm((Ly, D_MODEL, 2 * D_FF), D_MODEL ** -0.5),
        "ffn_conv": nrm((Ly, CONV_W, 2 * D_FF), CONV_W ** -0.5),
        "ffn_conv_b": nrm((Ly, 2 * D_FF), 0.02),
        "ffn_w_down": nrm((Ly, D_FF, D_MODEL), D_FF ** -0.5),
        "norm_final": gain((D_MODEL,)),
    }


def reference(x, mem, norm_mix, w_in, rw_mu, rw_w0, rw_w_up, rw_a0, rw_a_up, rw_g_up,
              rw_k_k, rw_k_a, rw_r_k, rw_ln_w, rw_ln_b,
              s5_a_re, s5_a_im, s5_b_re, s5_b_im, s5_c_re, s5_c_im, s5_d, s5_log_step,
              s5_w_glu, s5_b_glu, gla_gk_up, gla_gk_b, gla_norm, w_branch, w_out,
              norm_mem, norm_xattn, xa_wq, xa_wkv, xa_wo,
              norm_ffn, ffn_w_up, ffn_conv, ffn_conv_b, ffn_w_down, norm_final):
    B, L, _ = x.shape
    mem_n = rms_norm(mem, norm_mem)
    for l in range(DEPTH):
        h = rms_norm(x, norm_mix[l])
        p_rw, p_s5, p_gla, p_gate = _split(h @ w_in[l], IN_SIZES)
        y_rw = rwkv7_mix(p_rw, rw_mu[l], rw_w0[l], rw_w_up[l], rw_a0[l], rw_a_up[l], rw_g_up[l],
                         rw_k_k[l], rw_k_a[l], rw_r_k[l], rw_ln_w[l], rw_ln_b[l])
        y_s5 = s5_mix(p_s5, s5_a_re[l], s5_a_im[l], s5_b_re[l], s5_b_im[l], s5_c_re[l], s5_c_im[l],
                      s5_d[l], s5_log_step[l], s5_w_glu[l], s5_b_glu[l])
        y_gla = gla_mix(p_gla, gla_gk_up[l], gla_gk_b[l], gla_norm[l])
        gates = jax.nn.sigmoid(p_gate).reshape(B, L, N_BRANCH, D_MODEL)
        merged = (gates[:, :, 0] * (y_rw @ w_branch[l, 0])
                  + gates[:, :, 1] * (y_s5 @ w_branch[l, 1])
                  + gates[:, :, 2] * (y_gla @ w_branch[l, 2]))
        x = x + merged @ w_out[l]
        x = x + cross_attention(rms_norm(x, norm_xattn[l]), mem_n, xa_wq[l], xa_wkv[l], xa_wo[l])
        x = x + conv_ffn(rms_norm(x, norm_ffn[l]), ffn_w_up[l], ffn_conv[l], ffn_conv_b[l], ffn_w_down[l])
    return rms_norm(x, norm_final)
```

```python
import functools
import math

import jax
import jax.numpy as jnp
from jax import lax
from jax.experimental import pallas as pl
from jax.experimental.pallas import tpu as pltpu

F32 = jnp.float32
BF16 = jnp.bfloat16

D_MODEL = 1024
DEPTH = 4
N_BRANCH = 3
MIX_W = 512

RWKV_HEAD = 64
RWKV_HEADS = 8
RWKV_DECAY_RANK = 64
RWKV_A_RANK = 64
RWKV_G_RANK = 128
RWKV_GN_EPS = RWKV_HEAD * 1e-5
RWKV_COLS = 3 * MIX_W + RWKV_DECAY_RANK + RWKV_A_RANK + RWKV_G_RANK

S5_GROUP = 16
S5_GROUPS = 32
S5_STATE = 64
S5_COLS = MIX_W
S5_NS = S5_GROUPS * S5_STATE

GLA_HEADS = 4
GLA_DV = 128
GLA_DK = 64
GLA_GATE_RANK = 16
GLA_GATE_NORM = 16.0
GLA_QK = GLA_HEADS * GLA_DK
GLA_COLS = 2 * GLA_QK + 2 * MIX_W + GLA_GATE_RANK
GLA_COLS_PAD = 1664

XA_HEADS = 4
XA_HEAD = 256
D_FF = 2816
NORM_EPS = 1e-6

LANES = 128
CHUNK = 64
EXP_M05 = math.exp(-0.5)
GELU_C = math.sqrt(2.0 / math.pi)

VMEM_LIMIT = 56 * 1024 * 1024


def _mm(a, b):
    return jnp.dot(a.astype(BF16), b.astype(BF16), preferred_element_type=F32)


def _mm_nt(a, b):
    return lax.dot_general(a.astype(BF16), b.astype(BF16), (((1,), (1,)), ((), ())),
                           preferred_element_type=F32)


def _mm_tn(a, b):
    return lax.dot_general(a.astype(BF16), b.astype(BF16), (((0,), (0,)), ((), ())),
                           preferred_element_type=F32)


def _split(a):
    hi = a.astype(BF16)
    lo = (a - hi.astype(F32)).astype(BF16)
    return hi, lo


def _mm_split_l(a, b_exact):
    hi, lo = _split(a)
    return (jnp.dot(hi, b_exact, preferred_element_type=F32)
            + jnp.dot(lo, b_exact, preferred_element_type=F32))


def _mm_split_r(a_exact, b):
    hi, lo = _split(b)
    return (jnp.dot(a_exact, hi, preferred_element_type=F32)
            + jnp.dot(a_exact, lo, preferred_element_type=F32))


def _sigmoid(x):
    return 1.0 / (1.0 + jnp.exp(-x))


def _rms(x, g):
    ms = jnp.mean(x * x, axis=-1, keepdims=True)
    return x * lax.rsqrt(ms + NORM_EPS) * g


def _iota(shape, dim):
    return lax.broadcasted_iota(jnp.int32, shape, dim)


def _params(sem):
    return pltpu.CompilerParams(dimension_semantics=sem, vmem_limit_bytes=VMEM_LIMIT)


def _full(shape):
    nd = len(shape)
    return pl.BlockSpec(shape, lambda *_: (0,) * nd)


def _norm_mm_kernel(x_ref, g_ref, w_ref, o_ref):
    o_ref[...] = _mm(_rms(x_ref[...], g_ref[...]), w_ref[...]).astype(o_ref.dtype)


def _s5_proj(x, g, w):
    B, L, D = x.shape
    tt = min(L, 512)
    return pl.pallas_call(
        _norm_mm_kernel,
        grid=(L // tt, B),
        in_specs=[pl.BlockSpec((None, tt, D), lambda i, b: (b, i, 0)),
                  _full((1, D)), _full((D, S5_COLS))],
        out_specs=pl.BlockSpec((tt, S5_COLS), lambda i, b: (i, b)),
        out_shape=jax.ShapeDtypeStruct((L, B * S5_COLS), F32),
        compiler_params=_params(("parallel", "parallel")),
        name="s5_proj",
    )(x, g, w)


def _kv_proj(mem, g, w):
    B, M, D = mem.shape
    N = w.shape[1]
    return pl.pallas_call(
        _norm_mm_kernel,
        grid=(B,),
        in_specs=[pl.BlockSpec((None, M, D), lambda b: (b, 0, 0)),
                  _full((1, D)), _full((D, N))],
        out_specs=pl.BlockSpec((None, M, N), lambda b: (b, 0, 0)),
        out_shape=jax.ShapeDtypeStruct((B, M, N), BF16),
        compiler_params=_params(("parallel",)),
        name="kv_proj",
    )(mem, g, w)


def _s5_kernel(nb, u_ref, are_ref, aim_ref, bre_ref, bim_ref, cre_ref, cim_ref, d_ref,
               wglu_ref, bglu_ref, o_ref, sre, sim, st_re, st_im):
    rows = u_ref.shape[0]
    tt = rows // nb

    @pl.when(pl.program_id(0) == 0)
    def _():
        st_re[...] = jnp.zeros_like(st_re)
        st_im[...] = jnp.zeros_like(st_im)

    u = u_ref[...]
    half = S5_NS // 2
    for kt in range(2):
        uk = u[:, 256 * kt:256 * (kt + 1)].astype(BF16)
        sre[:, half * kt:half * (kt + 1)] = jnp.dot(uk, bre_ref[kt], preferred_element_type=F32)
        sim[:, half * kt:half * (kt + 1)] = jnp.dot(uk, bim_ref[kt], preferred_element_type=F32)

    cw = 512
    for cg in range(S5_NS // cw):
        cs = slice(cw * cg, cw * (cg + 1))
        ar = jnp.broadcast_to(are_ref[:, cs], (nb, cw))
        ai = jnp.broadcast_to(aim_ref[:, cs], (nb, cw))

        def body(t, carry):
            re, im = carry
            rs = pl.ds(pl.multiple_of(t * nb, nb), nb)
            nre = ar * re - ai * im + sre[rs, cs]
            nim = ar * im + ai * re + sim[rs, cs]
            sre[rs, cs] = nre
            sim[rs, cs] = nim
            return nre, nim

        re, im = lax.fori_loop(0, tt, body, (st_re[:, cs], st_im[:, cs]))
        st_re[:, cs] = re
        st_im[:, cs] = im

    ys = []
    for jt in range(2):
        hs = slice(half * jt, half * (jt + 1))
        ys.append(_mm(sre[:, hs], cre_ref[jt]) + _mm(sim[:, hs], cim_ref[jt]))
    y = jnp.concatenate(ys, axis=1) + d_ref[...] * u
    y = 0.5 * y * (1.0 + jnp.tanh(GELU_C * (y + 0.044715 * (y * y * y))))
    o_ref[...] = y * _sigmoid(_mm(y, wglu_ref[...]) + bglu_ref[...])


def _s5_mix(p_tm, nb, are, aim, bre, bim, cre, cim, d, wglu, bglu):
    rows_total = p_tm.shape[0]
    L = rows_total // nb
    tt = min(L, 32)
    rows = tt * nb
    return pl.pallas_call(
        functools.partial(_s5_kernel, nb),
        grid=(L // tt,),
        in_specs=[pl.BlockSpec((rows, S5_COLS), lambda i: (i, 0)),
                  _full(are.shape), _full(aim.shape), _full(bre.shape), _full(bim.shape),
                  _full(cre.shape), _full(cim.shape), _full(d.shape), _full(wglu.shape),
                  _full(bglu.shape)],
        out_specs=pl.BlockSpec((rows, S5_COLS), lambda i: (i, 0)),
        out_shape=jax.ShapeDtypeStruct((rows_total, S5_COLS), F32),
        scratch_shapes=[pltpu.VMEM((rows, S5_NS), F32), pltpu.VMEM((rows, S5_NS), F32),
                        pltpu.VMEM((nb, S5_NS), F32), pltpu.VMEM((nb, S5_NS), F32)],
        compiler_params=_params(("arbitrary",)),
        name="s5_mix",
    )(p_tm, are, aim, bre, bim, cre, cim, d, wglu, bglu)


def _s5_tables(a_re, a_im, b_re, b_im, c_re, c_im, log_step):
    dt = jnp.exp(log_step)[:, None]
    mag = jnp.exp(dt * a_re)
    ang = dt * a_im
    abar_re, abar_im = mag * jnp.cos(ang), mag * jnp.sin(ang)
    den = a_re * a_re + a_im * a_im
    nr, ni = abar_re - 1.0, abar_im
    f_re = ((nr * a_re + ni * a_im) / den)[..., None]
    f_im = ((ni * a_re - nr * a_im) / den)[..., None]
    bbar_re = f_re * b_re - f_im * b_im
    bbar_im = f_re * b_im + f_im * b_re
    gl = S5_GROUPS // 2
    eye = jnp.eye(gl, dtype=a_re.dtype)

    def bmat(bb):
        bb = bb.reshape(2, gl, S5_STATE, S5_GROUP)
        return jnp.einsum('kgnc,gh->kgchn', bb, eye).reshape(2, gl * S5_GROUP, gl * S5_STATE)

    def cmat(cc):
        cc = cc.reshape(2, gl, S5_GROUP, S5_STATE)
        return jnp.einsum('kgcn,gh->kgnhc', cc, eye).reshape(2, gl * S5_STATE, gl * S5_GROUP)

    return (abar_re.reshape(1, S5_NS), abar_im.reshape(1, S5_NS),
            bmat(bbar_re).astype(BF16), bmat(bbar_im).astype(BF16),
            cmat(c_re).astype(BF16), cmat(-c_im).astype(BF16))


def _rwkv_kernel(x_ref, g_ref, w_ref, mu_ref, w0_ref, lora_ref, a0_ref, gup_ref, kk_ref, ka_ref,
                 rk_ref, lnw_ref, lnb_ref, ones_ref, tri_ref, o_ref,
                 prev_ref, st_ref, ld_s, al_s, be_s, k_s, r_s, v_s, y_s, bon_s, g_s):
    tb = x_ref.shape[0]
    C = CHUNK

    @pl.when(pl.program_id(1) == 0)
    def _():
        prev_ref[...] = jnp.zeros_like(prev_ref)
        st_ref[...] = jnp.zeros_like(st_ref)

    p = _mm(_rms(x_ref[...], g_ref[...]), w_ref[...])
    shifted = jnp.where(_iota(p.shape, 0) == 0, prev_ref[0:1, :], pltpu.roll(p, 1, axis=0))
    prev_ref[0:1, :] = p[tb - 1:tb, :]
    p = p + (shifted - p) * mu_ref[...]

    r = p[:, 0:MIX_W]
    k = p[:, MIX_W:2 * MIX_W]
    v = p[:, 2 * MIX_W:3 * MIX_W]
    lin = p[:, 3 * MIX_W:3 * MIX_W + LANES]
    gd = p[:, 3 * MIX_W + LANES:]
    lin = jnp.where(_iota(lin.shape, 1) < RWKV_DECAY_RANK, jnp.tanh(lin), lin)
    proj = _mm(lin, lora_ref[...])
    ld_s[...] = -EXP_M05 * _sigmoid(w0_ref[...] + proj[:, :MIX_W])
    a = _sigmoid(a0_ref[...] + proj[:, MIX_W:])
    g_s[...] = _mm(_sigmoid(gd), gup_ref[...])
    kkv = k * kk_ref[...]
    kk = kkv * lax.rsqrt(_mm_split_l(kkv * kkv, ones_ref[...]) + 1e-12)
    k2 = k * (1.0 + (a - 1.0) * ka_ref[...])
    bon_s[...] = _mm_split_l(r * k2 * rk_ref[...], ones_ref[...]) * v
    al_s[...] = -kk
    be_s[...] = kk * a
    k_s[...] = k2
    r_s[...] = r
    v_s[...] = v

    m0 = _iota((C, LANES), 1) < RWKV_HEAD
    ri = _iota((2 * C, 2 * C), 0)
    ci = _iota((2 * C, 2 * C), 1)
    same = (ri >= C) == (ci >= C)
    strict = same & (ri > ci)
    incl = same & (ri >= ci)
    bd = (_iota((LANES, LANES), 0) >= RWKV_HEAD) == (_iota((LANES, LANES), 1) >= RWKV_HEAD)
    zc = jnp.zeros((C, LANES), F32)
    n_lvl = C.bit_length() - 1

    def chunk(c, carry):
        rs = pl.ds(pl.multiple_of(c * C, C), C)
        ldc = ld_s[rs, :]
        cum = _mm_split_r(tri_ref[...], ldc)
        tot = cum[C - 1:C, :]
        e_neg = jnp.exp(-cum)
        e_st = jnp.exp(tot - cum)
        al = al_s[rs, :] * jnp.exp(cum - ldc)
        be = be_s[rs, :]
        kc = k_s[rs, :]
        bt, bs = be * e_neg, be * e_st
        kt, ks = kc * e_neg, kc * e_st
        rt = r_s[rs, :] * jnp.exp(cum)
        vc = v_s[rs, :]
        wc = jnp.exp(tot)
        ys = []
        for j in range(RWKV_HEADS // 2):
            sl = slice(LANES * j, LANES * (j + 1))
            ap, rp, vp = al[:, sl], rt[:, sl], vc[:, sl]
            l4 = jnp.concatenate([jnp.where(m0, ap, 0.0), jnp.where(m0, 0.0, ap),
                                  jnp.where(m0, rp, 0.0), jnp.where(m0, 0.0, rp)], axis=0).astype(BF16)
            r4 = jnp.concatenate([bt[:, sl], bt[:, sl], kt[:, sl], kt[:, sl]], axis=0)
            gm = _mm_nt(l4, r4)
            n_ab = jnp.where(strict, gm[:2 * C, :2 * C], 0.0)
            a_ak = jnp.where(strict, gm[:2 * C, 2 * C:], 0.0)
            a_rb = jnp.where(incl, gm[2 * C:, :2 * C], 0.0)
            a_rk = jnp.where(incl, gm[2 * C:, 2 * C:], 0.0)
            st = st_ref[j].astype(BF16)
            v2 = jnp.concatenate([vp, vp], axis=0).astype(BF16)
            x = _mm(a_ak, v2) + _mm(l4[:2 * C], st)
            pw = n_ab
            for lvl in range(n_lvl):
                x = x + _mm(pw, x)
                if lvl + 1 < n_lvl:
                    pw = _mm(pw, pw)
            y2 = _mm(a_rb, x) + _mm(a_rk, v2)
            ys.append(jnp.where(m0, y2[:C], y2[C:]) + _mm(rp, st))
            lb = jnp.concatenate([jnp.where(m0, bs[:, sl], 0.0), jnp.where(m0, 0.0, bs[:, sl]),
                                  ks[:, sl], zc], axis=0)
            rb = jnp.concatenate([x, vp, zc], axis=0)
            ds = jnp.where(bd, _mm_tn(lb, rb), 0.0)
            wcol = jnp.broadcast_to(wc[:, sl], (LANES, LANES)).T
            st_ref[j] = wcol * st_ref[j] + ds
        y_s[rs, :] = jnp.concatenate(ys, axis=1)
        return carry

    lax.fori_loop(0, tb // C, chunk, 0)

    y = y_s[...]
    inv = 1.0 / RWKV_HEAD
    mean = _mm_split_l(y, ones_ref[...]) * inv
    dlt = y - mean
    var = _mm_split_l(dlt * dlt, ones_ref[...]) * inv
    yn = dlt * lax.rsqrt(var + RWKV_GN_EPS) * lnw_ref[...] + lnb_ref[...]
    o_ref[...] = (yn + bon_s[...]) * g_s[...]


def _rwkv_mix(x, g, w, mu, w0, lora, a0, gup, kk, ka, rk, lnw, lnb):
    B, L, D = x.shape
    tb = min(L, 256)
    ones = jnp.kron(jnp.eye(RWKV_HEADS, dtype=F32), jnp.ones((RWKV_HEAD, RWKV_HEAD), F32)).astype(BF16)
    tri = jnp.tril(jnp.ones((CHUNK, CHUNK), F32)).astype(BF16)
    consts = (g, w, mu, w0, lora, a0, gup, kk, ka, rk, lnw, lnb, ones, tri)
    act = lambda: pltpu.VMEM((tb, MIX_W), F32)
    return pl.pallas_call(
        _rwkv_kernel,
        grid=(B, L // tb),
        in_specs=[pl.BlockSpec((None, tb, D), lambda b, i: (b, i, 0))] + [_full(c.shape) for c in consts],
        out_specs=pl.BlockSpec((None, tb, MIX_W), lambda b, i: (b, i, 0)),
        out_shape=jax.ShapeDtypeStruct((B, L, MIX_W), F32),
        scratch_shapes=[pltpu.VMEM((8, RWKV_COLS), F32),
                        pltpu.VMEM((RWKV_HEADS // 2, LANES, LANES), F32)] + [act() for _ in range(9)],
        compiler_params=_params(("parallel", "arbitrary")),
        name="rwkv_mix",
    )(x, *consts)


def _gla_kernel(x_ref, g_ref, w_ref, gkup_ref, gkb_ref, ng_ref, tri_ref, o_ref,
                st_ref, q_s, k_s, v_s, gk_s, o_s):
    tb = x_ref.shape[0]
    C = CHUNK

    @pl.when(pl.program_id(1) == 0)
    def _():
        st_ref[...] = jnp.zeros_like(st_ref)

    p = _mm(_rms(x_ref[...], g_ref[...]), w_ref[...])
    q_s[...] = p[:, 0:GLA_QK] * (GLA_DK ** -0.5)
    k_s[...] = p[:, GLA_QK:2 * GLA_QK]
    v_s[...] = p[:, 2 * GLA_QK:2 * GLA_QK + MIX_W]
    go = p[:, 2 * GLA_QK + MIX_W:2 * GLA_QK + 2 * MIX_W]
    z = _mm(p[:, 2 * GLA_QK + 2 * MIX_W:], gkup_ref[...]) + gkb_ref[...]
    gk_s[...] = (jnp.minimum(z, 0.0) - jnp.log(1.0 + jnp.exp(-jnp.abs(z)))) * (1.0 / GLA_GATE_NORM)

    m0 = _iota((C, LANES), 1) < GLA_DK
    mv = _iota((C, 2 * GLA_DV), 1) < GLA_DV
    low = (_iota((2 * C, C), 0) % C) >= _iota((2 * C, C), 1)
    bd = (_iota((LANES, 2 * GLA_DV), 0) >= GLA_DK) == (_iota((LANES, 2 * GLA_DV), 1) >= GLA_DV)

    def chunk(c, carry):
        rs = pl.ds(pl.multiple_of(c * C, C), C)
        bcum = _mm_split_r(tri_ref[...], gk_s[rs, :])
        blast = bcum[C - 1:C, :]
        kc = k_s[rs, :]
        qd = q_s[rs, :] * jnp.exp(bcum)
        ki = kc * jnp.exp(-bcum)
        ks = kc * jnp.exp(blast - bcum)
        el = jnp.exp(blast)
        vc = v_s[rs, :]
        os_ = []
        for j in range(GLA_HEADS // 2):
            sl = slice(LANES * j, LANES * (j + 1))
            sv = slice(2 * GLA_DV * j, 2 * GLA_DV * (j + 1))
            qp, vp = qd[:, sl], vc[:, sv].astype(BF16)
            l2 = jnp.concatenate([jnp.where(m0, qp, 0.0), jnp.where(m0, 0.0, qp)], axis=0)
            att = jnp.where(low, _mm_nt(l2, ki[:, sl]), 0.0)
            o2 = _mm(att, vp)
            st = st_ref[j]
            os_.append(jnp.where(mv, o2[:C], o2[C:]) + _mm(qp, st))
            ds = jnp.where(bd, _mm_tn(ks[:, sl], vp), 0.0)
            ecol = jnp.broadcast_to(el[:, sl], (LANES, LANES)).T
            st_ref[j] = jnp.concatenate([ecol, ecol], axis=1) * st + ds
        o_s[rs, :] = jnp.concatenate(os_, axis=1)
        return carry

    lax.fori_loop(0, tb // C, chunk, 0)

    o = o_s[...]
    outs = []
    for h in range(GLA_HEADS):
        oh = o[:, GLA_DV * h:GLA_DV * (h + 1)]
        outs.append(oh * lax.rsqrt(jnp.mean(oh * oh, axis=-1, keepdims=True) + 1e-5) * ng_ref[...])
    o_ref[...] = jnp.concatenate(outs, axis=1) * (go * _sigmoid(go))


def _gla_mix(x, g, w, gkup, gkb, ng):
    B, L, D = x.shape
    tb = min(L, 256)
    tri = jnp.tril(jnp.ones((CHUNK, CHUNK), F32)).astype(BF16)
    consts = (g, w, gkup, gkb, ng, tri)
    return pl.pallas_call(
        _gla_kernel,
        grid=(B, L // tb),
        in_specs=[pl.BlockSpec((None, tb, D), lambda b, i: (b, i, 0))] + [_full(c.shape) for c in consts],
        out_specs=pl.BlockSpec((None, tb, MIX_W), lambda b, i: (b, i, 0)),
        out_shape=jax.ShapeDtypeStruct((B, L, MIX_W), F32),
        scratch_shapes=[pltpu.VMEM((GLA_HEADS // 2, LANES, 2 * GLA_DV), F32),
                        pltpu.VMEM((tb, GLA_QK), F32), pltpu.VMEM((tb, GLA_QK), F32),
                        pltpu.VMEM((tb, MIX_W), F32), pltpu.VMEM((tb, GLA_QK), F32),
                        pltpu.VMEM((tb, MIX_W), F32)],
        compiler_params=_params(("parallel", "arbitrary")),
        name="gla_mix",
    )(x, *consts)


def _merge_kernel(x_ref, yrw_ref, ys5_ref, ygla_ref, g_ref, wg_ref, wb_ref, wo_ref, o_ref):
    x = x_ref[...]
    gates = _sigmoid(_mm(_rms(x, g_ref[...]), wg_ref[...]))
    merged = (gates[:, 0:D_MODEL] * _mm(yrw_ref[...], wb_ref[0])
              + gates[:, D_MODEL:2 * D_MODEL] * _mm(ys5_ref[...], wb_ref[1])
              + gates[:, 2 * D_MODEL:] * _mm(ygla_ref[...], wb_ref[2]))
    o_ref[...] = x + _mm(merged, wo_ref[...])


def _merge(x, y_rw, y_s5_tm, y_gla, g, wg, wb, wo):
    B, L, D = x.shape
    tm = min(L, 512)
    bspec = lambda w: pl.BlockSpec((None, tm, w), lambda b, i: (b, i, 0))
    return pl.pallas_call(
        _merge_kernel,
        grid=(B, L // tm),
        in_specs=[bspec(D), bspec(MIX_W),
                  pl.BlockSpec((tm, MIX_W), lambda b, i: (i, b)),
                  bspec(MIX_W), _full(g.shape), _full(wg.shape), _full(wb.shape), _full(wo.shape)],
        out_specs=bspec(D),
        out_shape=jax.ShapeDtypeStruct((B, L, D), F32),
        compiler_params=_params(("parallel", "parallel")),
        name="merge",
    )(x, y_rw, y_s5_tm, y_gla, g, wg, wb, wo)


def _xattn_kernel(x_ref, kv_ref, g_ref, wq_ref, wo_ref, o_ref):
    x = x_ref[...]
    q = _mm(_rms(x, g_ref[...]), wq_ref[...])
    outs = []
    for h in range(XA_HEADS):
        kh = kv_ref[:, XA_HEAD * h:XA_HEAD * (h + 1)]
        vh = kv_ref[:, D_MODEL + XA_HEAD * h:D_MODEL + XA_HEAD * (h + 1)]
        s = _mm_nt(q[:, XA_HEAD * h:XA_HEAD * (h + 1)], kh) * (XA_HEAD ** -0.5)
        e = jnp.exp(s - jnp.max(s, axis=-1, keepdims=True))
        outs.append(_mm(e / jnp.sum(e, axis=-1, keepdims=True), vh))
    o_ref[...] = x + _mm(jnp.concatenate(outs, axis=1), wo_ref[...])


def _xattn(x, kv, g, wq, wo):
    B, L, D = x.shape
    M = kv.shape[1]
    tq = min(L, 512)
    xspec = pl.BlockSpec((None, tq, D), lambda b, i: (b, i, 0))
    return pl.pallas_call(
        _xattn_kernel,
        grid=(B, L // tq),
        in_specs=[xspec, pl.BlockSpec((None, M, 2 * D), lambda b, i: (b, 0, 0)),
                  _full(g.shape), _full(wq.shape), _full(wo.shape)],
        out_specs=xspec,
        out_shape=jax.ShapeDtypeStruct((B, L, D), F32),
        compiler_params=_params(("parallel", "parallel")),
        name="xattn",
    )(x, kv, g, wq, wo)


FFN_HALO = 16


def _ffn_kernel(x_ref, xh_ref, g_ref, wug_ref, wuv_ref, cwg_ref, cwv_ref, cbg_ref, cbv_ref, wd_ref,
                o_ref, hn_s, acc_s):
    j = pl.program_id(2)
    tm = x_ref.shape[0]

    @pl.when(j == 0)
    def _():
        halo = _rms(xh_ref[...], g_ref[...])
        halo = jnp.where(pl.program_id(1) == 0, 0.0, halo)
        hn_s[0:FFN_HALO, :] = halo.astype(BF16)
        hn_s[FFN_HALO:, :] = _rms(x_ref[...], g_ref[...]).astype(BF16)
        acc_s[...] = jnp.zeros_like(acc_s)

    hn = hn_s[...]

    def conv(w_ref, cw_ref, cb_ref):
        u = jnp.dot(hn, w_ref[...], preferred_element_type=F32)
        c = (cw_ref[2:3, :] * u + cw_ref[1:2, :] * pltpu.roll(u, 1, axis=0)
             + cw_ref[0:1, :] * pltpu.roll(u, 2, axis=0) + cb_ref[...])
        return c[FFN_HALO:, :]

    gate = conv(wug_ref, cwg_ref, cbg_ref)
    val = conv(wuv_ref, cwv_ref, cbv_ref)
    acc_s[...] += _mm(gate * _sigmoid(gate) * val, wd_ref[...])

    @pl.when(j == pl.num_programs(2) - 1)
    def _():
        o_ref[...] = x_ref[...] + acc_s[...]


def _ffn(x, g, w_up, conv_w, conv_b, w_down):
    B, L, D = x.shape
    tm = min(L, 1024)
    bn = 256
    nj = D_FF // bn
    hb = tm // FFN_HALO
    xspec = pl.BlockSpec((None, tm, D), lambda b, i, j: (b, i, 0))
    return pl.pallas_call(
        _ffn_kernel,
        grid=(B, L // tm, nj),
        in_specs=[xspec,
                  pl.BlockSpec((None, FFN_HALO, D), lambda b, i, j: (b, jnp.maximum(i * hb - 1, 0), 0)),
                  _full(g.shape),
                  pl.BlockSpec((D, bn), lambda b, i, j: (0, j)),
                  pl.BlockSpec((D, bn), lambda b, i, j: (0, nj + j)),
                  pl.BlockSpec((3, bn), lambda b, i, j: (0, j)),
                  pl.BlockSpec((3, bn), lambda b, i, j: (0, nj + j)),
                  pl.BlockSpec((1, bn), lambda b, i, j: (0, j)),
                  pl.BlockSpec((1, bn), lambda b, i, j: (0, nj + j)),
                  pl.BlockSpec((bn, D), lambda b, i, j: (j, 0))],
        out_specs=xspec,
        out_shape=jax.ShapeDtypeStruct((B, L, D), F32),
        scratch_shapes=[pltpu.VMEM((FFN_HALO + tm, D), BF16), pltpu.VMEM((tm, D), F32)],
        compiler_params=_params(("parallel", "parallel", "arbitrary")),
        name="conv_ffn",
    )(x, x, g, w_up, w_up, conv_w, conv_w, conv_b, conv_b, w_down)


def _final_norm_kernel(x_ref, g_ref, o_ref):
    o_ref[...] = _rms(x_ref[...], g_ref[...])


def _final_norm(x, g):
    B, L, D = x.shape
    tm = min(L, 1024)
    spec = pl.BlockSpec((None, tm, D), lambda b, i: (b, i, 0))
    return pl.pallas_call(
        _final_norm_kernel,
        grid=(B, L // tm),
        in_specs=[spec, _full(g.shape)],
        out_specs=spec,
        out_shape=jax.ShapeDtypeStruct((B, L, D), F32),
        compiler_params=_params(("parallel", "parallel")),
        name="final_norm",
    )(x, g)


def _row(v):
    return v.reshape(1, -1)


def kernel(x, mem, norm_mix, w_in, rw_mu, rw_w0, rw_w_up, rw_a0, rw_a_up, rw_g_up, rw_k_k, rw_k_a, rw_r_k, rw_ln_w, rw_ln_b, s5_a_re, s5_a_im, s5_b_re, s5_b_im, s5_c_re, s5_c_im, s5_d, s5_log_step, s5_w_glu, s5_b_glu, gla_gk_up, gla_gk_b, gla_norm, w_branch, w_out, norm_mem, norm_xattn, xa_wq, xa_wkv, xa_wo, norm_ffn, ffn_w_up, ffn_conv, ffn_conv_b, ffn_w_down, norm_final):
    B, L, D = x.shape
    o_s5 = RWKV_COLS
    o_gla = o_s5 + S5_COLS
    o_gate = o_gla + GLA_COLS
    for l in range(DEPTH):
        wl = w_in[l]
        g_mix = _row(norm_mix[l])
        tabs = _s5_tables(s5_a_re[l], s5_a_im[l], s5_b_re[l], s5_b_im[l], s5_c_re[l], s5_c_im[l],
                          s5_log_step[l])
        p_s5 = _s5_proj(x, g_mix, wl[:, o_s5:o_gla].astype(BF16))
        y_s5 = _s5_mix(p_s5.reshape(L * B, S5_COLS), B, *tabs, _row(s5_d[l]),
                       s5_w_glu[l].astype(BF16), _row(s5_b_glu[l])).reshape(L, B * S5_COLS)
        zeros = jnp.zeros((RWKV_DECAY_RANK, MIX_W), w_in.dtype)
        lora = jnp.concatenate([jnp.concatenate([rw_w_up[l], zeros], axis=1),
                                jnp.concatenate([zeros, rw_a_up[l]], axis=1)], axis=0).astype(BF16)
        y_rw = _rwkv_mix(x, g_mix, wl[:, :o_s5].astype(BF16), _row(rw_mu[l]), _row(rw_w0[l]), lora,
                         _row(rw_a0[l]), rw_g_up[l].astype(BF16), _row(rw_k_k[l]), _row(rw_k_a[l]),
                         _row(rw_r_k[l]), _row(rw_ln_w[l]), _row(rw_ln_b[l]))
        w_gla = wl[:, o_gla:o_gate]
        w_gla = jnp.concatenate(
            [w_gla[:, :2 * GLA_QK + MIX_W], w_gla[:, 2 * GLA_QK + MIX_W + GLA_GATE_RANK:],
             w_gla[:, 2 * GLA_QK + MIX_W:2 * GLA_QK + MIX_W + GLA_GATE_RANK],
             jnp.zeros((D, GLA_COLS_PAD - GLA_COLS), w_in.dtype)], axis=1).astype(BF16)
        gkup = jnp.concatenate([gla_gk_up[l], jnp.zeros((LANES - GLA_GATE_RANK, GLA_QK), w_in.dtype)],
                               axis=0).astype(BF16)
        y_gla = _gla_mix(x, g_mix, w_gla, gkup, _row(gla_gk_b[l]), _row(gla_norm[l]))
        x = _merge(x, y_rw, y_s5, y_gla, g_mix, wl[:, o_gate:].astype(BF16),
                   w_branch[l].astype(BF16), w_out[l].astype(BF16))
        kv = _kv_proj(mem, _row(norm_mem), xa_wkv[l].astype(BF16))
        x = _xattn(x, kv, _row(norm_xattn[l]), xa_wq[l].astype(BF16), xa_wo[l].astype(BF16))
        x = _ffn(x, _row(norm_ffn[l]), ffn_w_up[l].astype(BF16), ffn_conv[l], _row(ffn_conv_b[l]),
                 ffn_w_down[l].astype(BF16))
    return _final_norm(x, _row(norm_final))
```

```python
import functools
import math

import jax
import jax.numpy as jnp
from jax import lax
from jax.experimental import pallas as pl
from jax.experimental.pallas import tpu as pltpu

F32 = jnp.float32
BF16 = jnp.bfloat16

D_MODEL = 1024
DEPTH = 4
N_BRANCH = 3
MIX_W = 512

RWKV_HEAD = 64
RWKV_HEADS = 8
RWKV_DECAY_RANK = 64
RWKV_A_RANK = 64
RWKV_G_RANK = 128
RWKV_GN_EPS = RWKV_HEAD * 1e-5
RWKV_COLS = 3 * MIX_W + RWKV_DECAY_RANK + RWKV_A_RANK + RWKV_G_RANK

S5_GROUP = 16
S5_GROUPS = 32
S5_STATE = 64
S5_COLS = MIX_W
S5_NS = S5_GROUPS * S5_STATE

GLA_HEADS = 4
GLA_DV = 128
GLA_DK = 64
GLA_GATE_RANK = 16
GLA_GATE_NORM = 16.0
GLA_QK = GLA_HEADS * GLA_DK
GLA_COLS = 2 * GLA_QK + 2 * MIX_W + GLA_GATE_RANK
GLA_COLS_PAD = 1664

XA_HEADS = 4
XA_HEAD = 256
D_FF = 2816
NORM_EPS = 1e-6

LANES = 128
CHUNK = 64
EXP_M05 = math.exp(-0.5)
GELU_C = math.sqrt(2.0 / math.pi)

VMEM_LIMIT = 56 * 1024 * 1024


def _mm(a, b):
    return jnp.dot(a.astype(BF16), b.astype(BF16), preferred_element_type=F32)


def _mm_nt(a, b):
    return lax.dot_general(a.astype(BF16), b.astype(BF16), (((1,), (1,)), ((), ())),
                           preferred_element_type=F32)


def _mm_tn(a, b):
    return lax.dot_general(a.astype(BF16), b.astype(BF16), (((0,), (0,)), ((), ())),
                           preferred_element_type=F32)


def _split(a):
    hi = a.astype(BF16)
    lo = (a - hi.astype(F32)).astype(BF16)
    return hi, lo


def _mm_split_l(a, b_exact):
    hi, lo = _split(a)
    return (jnp.dot(hi, b_exact, preferred_element_type=F32)
            + jnp.dot(lo, b_exact, preferred_element_type=F32))


def _mm_split_r(a_exact, b):
    hi, lo = _split(b)
    return (jnp.dot(a_exact, hi, preferred_element_type=F32)
            + jnp.dot(a_exact, lo, preferred_element_type=F32))


def _sigmoid(x):
    return 1.0 / (1.0 + jnp.exp(-x))


def _rms(x, g):
    ms = jnp.mean(x * x, axis=-1, keepdims=True)
    return x * lax.rsqrt(ms + NORM_EPS) * g


def _iota(shape, dim):
    return lax.broadcasted_iota(jnp.int32, shape, dim)


def _params(sem):
    return pltpu.CompilerParams(dimension_semantics=sem, vmem_limit_bytes=VMEM_LIMIT)


def _full(shape):
    nd = len(shape)
    return pl.BlockSpec(shape, lambda *_: (0,) * nd)


def _norm_mm_kernel(x_ref, g_ref, w_ref, o_ref):
    o_ref[...] = _mm(_rms(x_ref[...], g_ref[...]), w_ref[...]).astype(o_ref.dtype)


def _s5_proj(x, g, w):
    B, L, D = x.shape
    tt = min(L, 512)
    return pl.pallas_call(
        _norm_mm_kernel,
        grid=(L // tt, B),
        in_specs=[pl.BlockSpec((None, tt, D), lambda i, b: (b, i, 0)),
                  _full((1, D)), _full((D, S5_COLS))],
        out_specs=pl.BlockSpec((tt, S5_COLS), lambda i, b: (i, b)),
        out_shape=jax.ShapeDtypeStruct((L, B * S5_COLS), F32),
        compiler_params=_params(("parallel", "parallel")),
        name="s5_proj",
    )(x, g, w)


def _kv_proj(mem, g, w):
    B, M, D = mem.shape
    N = w.shape[1]
    return pl.pallas_call(
        _norm_mm_kernel,
        grid=(B,),
        in_specs=[pl.BlockSpec((None, M, D), lambda b: (b, 0, 0)),
                  _full((1, D)), _full((D, N))],
        out_specs=pl.BlockSpec((None, M, N), lambda b: (b, 0, 0)),
        out_shape=jax.ShapeDtypeStruct((B, M, N), BF16),
        compiler_params=_params(("parallel",)),
        name="kv_proj",
    )(mem, g, w)


def _s5_kernel(nb, u_ref, are_ref, aim_ref, bre_ref, bim_ref, cre_ref, cim_ref, d_ref,
               wglu_ref, bglu_ref, o_ref, sre, sim, st_re, st_im):
    rows = u_ref.shape[0]
    tt = rows // nb

    @pl.when(pl.program_id(0) == 0)
    def _():
        st_re[...] = jnp.zeros_like(st_re)
        st_im[...] = jnp.zeros_like(st_im)

    u = u_ref[...]
    half = S5_NS // 2
    for kt in range(2):
        uk = u[:, 256 * kt:256 * (kt + 1)].astype(BF16)
        sre[:, half * kt:half * (kt + 1)] = jnp.dot(uk, bre_ref[kt], preferred_element_type=F32)
        sim[:, half * kt:half * (kt + 1)] = jnp.dot(uk, bim_ref[kt], preferred_element_type=F32)

    cw = 512
    for cg in range(S5_NS // cw):
        cs = slice(cw * cg, cw * (cg + 1))
        ar = jnp.broadcast_to(are_ref[:, cs], (nb, cw))
        ai = jnp.broadcast_to(aim_ref[:, cs], (nb, cw))

        def body(t, carry):
            re, im = carry
            rs = pl.ds(pl.multiple_of(t * nb, nb), nb)
            nre = ar * re - ai * im + sre[rs, cs]
            nim = ar * im + ai * re + sim[rs, cs]
            sre[rs, cs] = nre
            sim[rs, cs] = nim
            return nre, nim

        re, im = lax.fori_loop(0, tt, body, (st_re[:, cs], st_im[:, cs]))
        st_re[:, cs] = re
        st_im[:, cs] = im

    ys = []
    for jt in range(2):
        hs = slice(half * jt, half * (jt + 1))
        ys.append(_mm(sre[:, hs], cre_ref[jt]) + _mm(sim[:, hs], cim_ref[jt]))
    y = jnp.concatenate(ys, axis=1) + d_ref[...] * u
    y = 0.5 * y * (1.0 + jnp.tanh(GELU_C * (y + 0.044715 * (y * y * y))))
    o_ref[...] = y * _sigmoid(_mm(y, wglu_ref[...]) + bglu_ref[...])


def _s5_mix(p_tm, nb, are, aim, bre, bim, cre, cim, d, wglu, bglu):
    rows_total = p_tm.shape[0]
    L = rows_total // nb
    tt = min(L, 32)
    rows = tt * nb
    return pl.pallas_call(
        functools.partial(_s5_kernel, nb),
        grid=(L // tt,),
        in_specs=[pl.BlockSpec((rows, S5_COLS), lambda i: (i, 0)),
                  _full(are.shape), _full(aim.shape), _full(bre.shape), _full(bim.shape),
                  _full(cre.shape), _full(cim.shape), _full(d.shape), _full(wglu.shape),
                  _full(bglu.shape)],
        out_specs=pl.BlockSpec((rows, S5_COLS), lambda i: (i, 0)),
        out_shape=jax.ShapeDtypeStruct((rows_total, S5_COLS), F32),
        scratch_shapes=[pltpu.VMEM((rows, S5_NS), F32), pltpu.VMEM((rows, S5_NS), F32),
                        pltpu.VMEM((nb, S5_NS), F32), pltpu.VMEM((nb, S5_NS), F32)],
        compiler_params=_params(("arbitrary",)),
        name="s5_mix",
    )(p_tm, are, aim, bre, bim, cre, cim, d, wglu, bglu)


def _s5_tables(a_re, a_im, b_re, b_im, c_re, c_im, log_step):
    dt = jnp.exp(log_step)[:, None]
    mag = jnp.exp(dt * a_re)
    ang = dt * a_im
    abar_re, abar_im = mag * jnp.cos(ang), mag * jnp.sin(ang)
    den = a_re * a_re + a_im * a_im
    nr, ni = abar_re - 1.0, abar_im
    f_re = ((nr * a_re + ni * a_im) / den)[..., None]
    f_im = ((ni * a_re - nr * a_im) / den)[..., None]
    bbar_re = f_re * b_re - f_im * b_im
    bbar_im = f_re * b_im + f_im * b_re
    gl = S5_GROUPS // 2
    eye = jnp.eye(gl, dtype=a_re.dtype)

    def bmat(bb):
        bb = bb.reshape(2, gl, S5_STATE, S5_GROUP)
        return jnp.einsum('kgnc,gh->kgchn', bb, eye).reshape(2, gl * S5_GROUP, gl * S5_STATE)

    def cmat(cc):
        cc = cc.reshape(2, gl, S5_GROUP, S5_STATE)
        return jnp.einsum('kgcn,gh->kgnhc', cc, eye).reshape(2, gl * S5_STATE, gl * S5_GROUP)

    return (abar_re.reshape(1, S5_NS), abar_im.reshape(1, S5_NS),
            bmat(bbar_re).astype(BF16), bmat(bbar_im).astype(BF16),
            cmat(c_re).astype(BF16), cmat(-c_im).astype(BF16))


def _rwkv_kernel(x_ref, g_ref, w_ref, mu_ref, w0_ref, lora_ref, a0_ref, gup_ref, kk_ref, ka_ref,
                 rk_ref, lnw_ref, lnb_ref, ones_ref, tri_ref, o_ref,
                 prev_ref, st_ref, ld_s, al_s, be_s, k_s, r_s, v_s, y_s, bon_s, g_s):
    tb = x_ref.shape[0]
    C = CHUNK

    @pl.when(pl.program_id(1) == 0)
    def _():
        prev_ref[...] = jnp.zeros_like(prev_ref)
        st_ref[...] = jnp.zeros_like(st_ref)

    p = _mm(_rms(x_ref[...], g_ref[...]), w_ref[...])
    shifted = jnp.where(_iota(p.shape, 0) == 0, prev_ref[0:1, :], pltpu.roll(p, 1, axis=0))
    prev_ref[0:1, :] = p[tb - 1:tb, :]
    p = p + (shifted - p) * mu_ref[...]

    r = p[:, 0:MIX_W]
    k = p[:, MIX_W:2 * MIX_W]
    v = p[:, 2 * MIX_W:3 * MIX_W]
    lin = p[:, 3 * MIX_W:3 * MIX_W + LANES]
    gd = p[:, 3 * MIX_W + LANES:]
    lin = jnp.where(_iota(lin.shape, 1) < RWKV_DECAY_RANK, jnp.tanh(lin), lin)
    proj = _mm(lin, lora_ref[...])
    ld_s[...] = -EXP_M05 * _sigmoid(w0_ref[...] + proj[:, :MIX_W])
    a = _sigmoid(a0_ref[...] + proj[:, MIX_W:])
    g_s[...] = _mm(_sigmoid(gd), gup_ref[...])
    kkv = k * kk_ref[...]
    kk = kkv * lax.rsqrt(_mm_split_l(kkv * kkv, ones_ref[...]) + 1e-12)
    k2 = k * (1.0 + (a - 1.0) * ka_ref[...])
    bon_s[...] = _mm_split_l(r * k2 * rk_ref[...], ones_ref[...]) * v
    al_s[...] = -kk
    be_s[...] = kk * a
    k_s[...] = k2
    r_s[...] = r
    v_s[...] = v

    m0 = _iota((C, LANES), 1) < RWKV_HEAD
    ri = _iota((2 * C, 2 * C), 0)
    ci = _iota((2 * C, 2 * C), 1)
    same = (ri >= C) == (ci >= C)
    strict = same & (ri > ci)
    incl = same & (ri >= ci)
    bd = (_iota((LANES, LANES), 0) >= RWKV_HEAD) == (_iota((LANES, LANES), 1) >= RWKV_HEAD)
    zc = jnp.zeros((C, LANES), F32)
    n_lvl = C.bit_length() - 1

    def chunk(c, carry):
        rs = pl.ds(pl.multiple_of(c * C, C), C)
        ldc = ld_s[rs, :]
        cum = _mm_split_r(tri_ref[...], ldc)
        tot = cum[C - 1:C, :]
        e_neg = jnp.exp(-cum)
        e_st = jnp.exp(tot - cum)
        al = al_s[rs, :] * jnp.exp(cum - ldc)
        be = be_s[rs, :]
        kc = k_s[rs, :]
        bt, bs = be * e_neg, be * e_st
        kt, ks = kc * e_neg, kc * e_st
        rt = r_s[rs, :] * jnp.exp(cum)
        vc = v_s[rs, :]
        wc = jnp.exp(tot)
        pairs = range(RWKV_HEADS // 2)
        sls = [slice(LANES * j, LANES * (j + 1)) for j in pairs]
        l4s, gms, sts, v2s = [], [], [], []
        for sl in sls:
            ap, rp = al[:, sl], rt[:, sl]
            l4 = jnp.concatenate([jnp.where(m0, ap, 0.0), jnp.where(m0, 0.0, ap),
                                  jnp.where(m0, rp, 0.0), jnp.where(m0, 0.0, rp)], axis=0).astype(BF16)
            r4 = jnp.concatenate([bt[:, sl], bt[:, sl], kt[:, sl], kt[:, sl]], axis=0)
            l4s.append(l4)
            gms.append(_mm_nt(l4, r4))
        for j in pairs:
            sts.append(st_ref[j].astype(BF16))
            v2s.append(jnp.concatenate([vc[:, sls[j]], vc[:, sls[j]]], axis=0).astype(BF16))
        xs = [_mm(jnp.where(strict, gms[j][:2 * C, 2 * C:], 0.0), v2s[j]) + _mm(l4s[j][:2 * C], sts[j])
              for j in pairs]
        pws = [jnp.where(strict, gms[j][:2 * C, :2 * C], 0.0).astype(BF16) for j in pairs]
        for lvl in range(n_lvl):
            xs = [xs[j] + _mm(pws[j], xs[j]) for j in pairs]
            if lvl + 1 < n_lvl:
                pws = [_mm(pws[j], pws[j]).astype(BF16) for j in pairs]
        ys = []
        for j in pairs:
            y2 = (_mm(jnp.where(incl, gms[j][2 * C:, :2 * C], 0.0), xs[j])
                  + _mm(jnp.where(incl, gms[j][2 * C:, 2 * C:], 0.0), v2s[j]))
            ys.append(jnp.where(m0, y2[:C], y2[C:]) + _mm(rt[:, sls[j]], sts[j]))
        y_s[rs, :] = jnp.concatenate(ys, axis=1)
        for j in pairs:
            sl = sls[j]
            lb = jnp.concatenate([jnp.where(m0, bs[:, sl], 0.0), jnp.where(m0, 0.0, bs[:, sl]),
                                  ks[:, sl], zc], axis=0)
            rb = jnp.concatenate([xs[j], vc[:, sl], zc], axis=0)
            ds = jnp.where(bd, _mm_tn(lb, rb), 0.0)
            wcol = jnp.broadcast_to(wc[:, sl], (LANES, LANES)).T
            st_ref[j] = wcol * st_ref[j] + ds
        return carry

    lax.fori_loop(0, tb // C, chunk, 0)

    y = y_s[...]
    inv = 1.0 / RWKV_HEAD
    mean = _mm_split_l(y, ones_ref[...]) * inv
    dlt = y - mean
    var = _mm_split_l(dlt * dlt, ones_ref[...]) * inv
    yn = dlt * lax.rsqrt(var + RWKV_GN_EPS) * lnw_ref[...] + lnb_ref[...]
    o_ref[...] = (yn + bon_s[...]) * g_s[...]


def _rwkv_mix(x, g, w, mu, w0, lora, a0, gup, kk, ka, rk, lnw, lnb):
    B, L, D = x.shape
    tb = min(L, 256)
    ones = jnp.kron(jnp.eye(RWKV_HEADS, dtype=F32), jnp.ones((RWKV_HEAD, RWKV_HEAD), F32)).astype(BF16)
    tri = jnp.tril(jnp.ones((CHUNK, CHUNK), F32)).astype(BF16)
    consts = (g, w, mu, w0, lora, a0, gup, kk, ka, rk, lnw, lnb, ones, tri)
    act = lambda: pltpu.VMEM((tb, MIX_W), F32)
    return pl.pallas_call(
        _rwkv_kernel,
        grid=(B, L // tb),
        in_specs=[pl.BlockSpec((None, tb, D), lambda b, i: (b, i, 0))] + [_full(c.shape) for c in consts],
        out_specs=pl.BlockSpec((None, tb, MIX_W), lambda b, i: (b, i, 0)),
        out_shape=jax.ShapeDtypeStruct((B, L, MIX_W), F32),
        scratch_shapes=[pltpu.VMEM((8, RWKV_COLS), F32),
                        pltpu.VMEM((RWKV_HEADS // 2, LANES, LANES), F32)] + [act() for _ in range(9)],
        compiler_params=_params(("parallel", "arbitrary")),
        name="rwkv_mix",
    )(x, *consts)


def _gla_kernel(x_ref, g_ref, w_ref, gkup_ref, gkb_ref, ng_ref, tri_ref, o_ref,
                st_ref, q_s, k_s, v_s, gk_s, o_s):
    tb = x_ref.shape[0]
    C = CHUNK

    @pl.when(pl.program_id(1) == 0)
    def _():
        st_ref[...] = jnp.zeros_like(st_ref)

    p = _mm(_rms(x_ref[...], g_ref[...]), w_ref[...])
    q_s[...] = p[:, 0:GLA_QK] * (GLA_DK ** -0.5)
    k_s[...] = p[:, GLA_QK:2 * GLA_QK]
    v_s[...] = p[:, 2 * GLA_QK:2 * GLA_QK + MIX_W]
    go = p[:, 2 * GLA_QK + MIX_W:2 * GLA_QK + 2 * MIX_W]
    z = _mm(p[:, 2 * GLA_QK + 2 * MIX_W:], gkup_ref[...]) + gkb_ref[...]
    gk_s[...] = (jnp.minimum(z, 0.0) - jnp.log(1.0 + jnp.exp(-jnp.abs(z)))) * (1.0 / GLA_GATE_NORM)

    m0 = _iota((C, LANES), 1) < GLA_DK
    mv = _iota((C, 2 * GLA_DV), 1) < GLA_DV
    low = (_iota((2 * C, C), 0) % C) >= _iota((2 * C, C), 1)
    bd = (_iota((LANES, 2 * GLA_DV), 0) >= GLA_DK) == (_iota((LANES, 2 * GLA_DV), 1) >= GLA_DV)

    def chunk(c, carry):
        rs = pl.ds(pl.multiple_of(c * C, C), C)
        bcum = _mm_split_r(tri_ref[...], gk_s[rs, :])
        blast = bcum[C - 1:C, :]
        kc = k_s[rs, :]
        qd = q_s[rs, :] * jnp.exp(bcum)
        ki = kc * jnp.exp(-bcum)
        ks = kc * jnp.exp(blast - bcum)
        el = jnp.exp(blast)
        vc = v_s[rs, :]
        pairs = range(GLA_HEADS // 2)
        sls = [slice(LANES * j, LANES * (j + 1)) for j in pairs]
        vps = [vc[:, 2 * GLA_DV * j:2 * GLA_DV * (j + 1)].astype(BF16) for j in pairs]
        atts = []
        for sl in sls:
            qp = qd[:, sl]
            l2 = jnp.concatenate([jnp.where(m0, qp, 0.0), jnp.where(m0, 0.0, qp)], axis=0)
            atts.append(jnp.where(low, _mm_nt(l2, ki[:, sl]), 0.0))
        sts = [st_ref[j] for j in pairs]
        qss = [_mm(qd[:, sls[j]], sts[j]) for j in pairs]
        dss = [jnp.where(bd, _mm_tn(ks[:, sls[j]], vps[j]), 0.0) for j in pairs]
        o2s = [_mm(atts[j], vps[j]) for j in pairs]
        o_s[rs, :] = jnp.concatenate([jnp.where(mv, o2s[j][:C], o2s[j][C:]) + qss[j] for j in pairs], axis=1)
        for j in pairs:
            ecol = jnp.broadcast_to(el[:, sls[j]], (LANES, LANES)).T
            st_ref[j] = jnp.concatenate([ecol, ecol], axis=1) * sts[j] + dss[j]
        return carry

    lax.fori_loop(0, tb // C, chunk, 0)

    o = o_s[...]
    outs = []
    for h in range(GLA_HEADS):
        oh = o[:, GLA_DV * h:GLA_DV * (h + 1)]
        outs.append(oh * lax.rsqrt(jnp.mean(oh * oh, axis=-1, keepdims=True) + 1e-5) * ng_ref[...])
    o_ref[...] = jnp.concatenate(outs, axis=1) * (go * _sigmoid(go))


def _gla_mix(x, g, w, gkup, gkb, ng):
    B, L, D = x.shape
    tb = min(L, 256)
    tri = jnp.tril(jnp.ones((CHUNK, CHUNK), F32)).astype(BF16)
    consts = (g, w, gkup, gkb, ng, tri)
    return pl.pallas_call(
        _gla_kernel,
        grid=(B, L // tb),
        in_specs=[pl.BlockSpec((None, tb, D), lambda b, i: (b, i, 0))] + [_full(c.shape) for c in consts],
        out_specs=pl.BlockSpec((None, tb, MIX_W), lambda b, i: (b, i, 0)),
        out_shape=jax.ShapeDtypeStruct((B, L, MIX_W), F32),
        scratch_shapes=[pltpu.VMEM((GLA_HEADS // 2, LANES, 2 * GLA_DV), F32),
                        pltpu.VMEM((tb, GLA_QK), F32), pltpu.VMEM((tb, GLA_QK), F32),
                        pltpu.VMEM((tb, MIX_W), F32), pltpu.VMEM((tb, GLA_QK), F32),
                        pltpu.VMEM((tb, MIX_W), F32)],
        compiler_params=_params(("parallel", "arbitrary")),
        name="gla_mix",
    )(x, *consts)


def _merge_kernel(x_ref, yrw_ref, ys5_ref, ygla_ref, g_ref, wg_ref, wb_ref, wo_ref, o_ref):
    x = x_ref[...]
    gates = _sigmoid(_mm(_rms(x, g_ref[...]), wg_ref[...]))
    merged = (gates[:, 0:D_MODEL] * _mm(yrw_ref[...], wb_ref[0])
              + gates[:, D_MODEL:2 * D_MODEL] * _mm(ys5_ref[...], wb_ref[1])
              + gates[:, 2 * D_MODEL:] * _mm(ygla_ref[...], wb_ref[2]))
    o_ref[...] = x + _mm(merged, wo_ref[...])


def _merge(x, y_rw, y_s5_tm, y_gla, g, wg, wb, wo):
    B, L, D = x.shape
    tm = min(L, 512)
    bspec = lambda w: pl.BlockSpec((None, tm, w), lambda b, i: (b, i, 0))
    return pl.pallas_call(
        _merge_kernel,
        grid=(B, L // tm),
        in_specs=[bspec(D), bspec(MIX_W),
                  pl.BlockSpec((tm, MIX_W), lambda b, i: (i, b)),
                  bspec(MIX_W), _full(g.shape), _full(wg.shape), _full(wb.shape), _full(wo.shape)],
        out_specs=bspec(D),
        out_shape=jax.ShapeDtypeStruct((B, L, D), F32),
        compiler_params=_params(("parallel", "parallel")),
        name="merge",
    )(x, y_rw, y_s5_tm, y_gla, g, wg, wb, wo)


def _xattn_kernel(x_ref, kv_ref, g_ref, wq_ref, wo_ref, o_ref):
    x = x_ref[...]
    q = _mm(_rms(x, g_ref[...]), wq_ref[...])
    outs = []
    for h in range(XA_HEADS):
        kh = kv_ref[:, XA_HEAD * h:XA_HEAD * (h + 1)]
        vh = kv_ref[:, D_MODEL + XA_HEAD * h:D_MODEL + XA_HEAD * (h + 1)]
        s = _mm_nt(q[:, XA_HEAD * h:XA_HEAD * (h + 1)], kh) * (XA_HEAD ** -0.5)
        e = jnp.exp(s - jnp.max(s, axis=-1, keepdims=True))
        outs.append(_mm(e / jnp.sum(e, axis=-1, keepdims=True), vh))
    o_ref[...] = x + _mm(jnp.concatenate(outs, axis=1), wo_ref[...])


def _xattn(x, kv, g, wq, wo):
    B, L, D = x.shape
    M = kv.shape[1]
    tq = min(L, 512)
    xspec = pl.BlockSpec((None, tq, D), lambda b, i: (b, i, 0))
    return pl.pallas_call(
        _xattn_kernel,
        grid=(B, L // tq),
        in_specs=[xspec, pl.BlockSpec((None, M, 2 * D), lambda b, i: (b, 0, 0)),
                  _full(g.shape), _full(wq.shape), _full(wo.shape)],
        out_specs=xspec,
        out_shape=jax.ShapeDtypeStruct((B, L, D), F32),
        compiler_params=_params(("parallel", "parallel")),
        name="xattn",
    )(x, kv, g, wq, wo)


FFN_HALO = 16


def _ffn_kernel(x_ref, xh_ref, g_ref, wug_ref, wuv_ref, cwg_ref, cwv_ref, cbg_ref, cbv_ref, wd_ref,
                o_ref, hn_s, acc_s):
    j = pl.program_id(2)
    tm = x_ref.shape[0]

    @pl.when(j == 0)
    def _():
        halo = _rms(xh_ref[...], g_ref[...])
        halo = jnp.where(pl.program_id(1) == 0, 0.0, halo)
        hn_s[0:FFN_HALO, :] = halo.astype(BF16)
        hn_s[FFN_HALO:, :] = _rms(x_ref[...], g_ref[...]).astype(BF16)
        acc_s[...] = jnp.zeros_like(acc_s)

    hn = hn_s[...]

    def conv(w_ref, cw_ref, cb_ref):
        u = jnp.dot(hn, w_ref[...], preferred_element_type=F32)
        c = (cw_ref[2:3, :] * u + cw_ref[1:2, :] * pltpu.roll(u, 1, axis=0)
             + cw_ref[0:1, :] * pltpu.roll(u, 2, axis=0) + cb_ref[...])
        return c[FFN_HALO:, :]

    gate = conv(wug_ref, cwg_ref, cbg_ref)
    val = conv(wuv_ref, cwv_ref, cbv_ref)
    acc_s[...] += _mm(gate * _sigmoid(gate) * val, wd_ref[...])

    @pl.when(j == pl.num_programs(2) - 1)
    def _():
        o_ref[...] = x_ref[...] + acc_s[...]


def _ffn(x, g, w_up, conv_w, conv_b, w_down):
    B, L, D = x.shape
    tm = min(L, 1024)
    bn = 256
    nj = D_FF // bn
    hb = tm // FFN_HALO
    xspec = pl.BlockSpec((None, tm, D), lambda b, i, j: (b, i, 0))
    return pl.pallas_call(
        _ffn_kernel,
        grid=(B, L // tm, nj),
        in_specs=[xspec,
                  pl.BlockSpec((None, FFN_HALO, D), lambda b, i, j: (b, jnp.maximum(i * hb - 1, 0), 0)),
                  _full(g.shape),
                  pl.BlockSpec((D, bn), lambda b, i, j: (0, j)),
                  pl.BlockSpec((D, bn), lambda b, i, j: (0, nj + j)),
                  pl.BlockSpec((3, bn), lambda b, i, j: (0, j)),
                  pl.BlockSpec((3, bn), lambda b, i, j: (0, nj + j)),
                  pl.BlockSpec((1, bn), lambda b, i, j: (0, j)),
                  pl.BlockSpec((1, bn), lambda b, i, j: (0, nj + j)),
                  pl.BlockSpec((bn, D), lambda b, i, j: (j, 0))],
        out_specs=xspec,
        out_shape=jax.ShapeDtypeStruct((B, L, D), F32),
        scratch_shapes=[pltpu.VMEM((FFN_HALO + tm, D), BF16), pltpu.VMEM((tm, D), F32)],
        compiler_params=_params(("parallel", "parallel", "arbitrary")),
        name="conv_ffn",
    )(x, x, g, w_up, w_up, conv_w, conv_w, conv_b, conv_b, w_down)


def _final_norm_kernel(x_ref, g_ref, o_ref):
    o_ref[...] = _rms(x_ref[...], g_ref[...])


def _final_norm(x, g):
    B, L, D = x.shape
    tm = min(L, 1024)
    spec = pl.BlockSpec((None, tm, D), lambda b, i: (b, i, 0))
    return pl.pallas_call(
        _final_norm_kernel,
        grid=(B, L // tm),
        in_specs=[spec, _full(g.shape)],
        out_specs=spec,
        out_shape=jax.ShapeDtypeStruct((B, L, D), F32),
        compiler_params=_params(("parallel", "parallel")),
        name="final_norm",
    )(x, g)


def _row(v):
    return v.reshape(1, -1)


def kernel(x, mem, norm_mix, w_in, rw_mu, rw_w0, rw_w_up, rw_a0, rw_a_up, rw_g_up, rw_k_k, rw_k_a, rw_r_k, rw_ln_w, rw_ln_b, s5_a_re, s5_a_im, s5_b_re, s5_b_im, s5_c_re, s5_c_im, s5_d, s5_log_step, s5_w_glu, s5_b_glu, gla_gk_up, gla_gk_b, gla_norm, w_branch, w_out, norm_mem, norm_xattn, xa_wq, xa_wkv, xa_wo, norm_ffn, ffn_w_up, ffn_conv, ffn_conv_b, ffn_w_down, norm_final):
    B, L, D = x.shape
    o_s5 = RWKV_COLS
    o_gla = o_s5 + S5_COLS
    o_gate = o_gla + GLA_COLS
    for l in range(DEPTH):
        wl = w_in[l]
        g_mix = _row(norm_mix[l])
        tabs = _s5_tables(s5_a_re[l], s5_a_im[l], s5_b_re[l], s5_b_im[l], s5_c_re[l], s5_c_im[l],
                          s5_log_step[l])
        p_s5 = _s5_proj(x, g_mix, wl[:, o_s5:o_gla].astype(BF16))
        y_s5 = _s5_mix(p_s5.reshape(L * B, S5_COLS), B, *tabs, _row(s5_d[l]),
                       s5_w_glu[l].astype(BF16), _row(s5_b_glu[l])).reshape(L, B * S5_COLS)
        zeros = jnp.zeros((RWKV_DECAY_RANK, MIX_W), w_in.dtype)
        lora = jnp.concatenate([jnp.concatenate([rw_w_up[l], zeros], axis=1),
                                jnp.concatenate([zeros, rw_a_up[l]], axis=1)], axis=0).astype(BF16)
        y_rw = _rwkv_mix(x, g_mix, wl[:, :o_s5].astype(BF16), _row(rw_mu[l]), _row(rw_w0[l]), lora,
                         _row(rw_a0[l]), rw_g_up[l].astype(BF16), _row(rw_k_k[l]), _row(rw_k_a[l]),
                         _row(rw_r_k[l]), _row(rw_ln_w[l]), _row(rw_ln_b[l]))
        w_gla = wl[:, o_gla:o_gate]
        w_gla = jnp.concatenate(
            [w_gla[:, :2 * GLA_QK + MIX_W], w_gla[:, 2 * GLA_QK + MIX_W + GLA_GATE_RANK:],
             w_gla[:, 2 * GLA_QK + MIX_W:2 * GLA_QK + MIX_W + GLA_GATE_RANK],
             jnp.zeros((D, GLA_COLS_PAD - GLA_COLS), w_in.dtype)], axis=1).astype(BF16)
        gkup = jnp.concatenate([gla_gk_up[l], jnp.zeros((LANES - GLA_GATE_RANK, GLA_QK), w_in.dtype)],
                               axis=0).astype(BF16)
        y_gla = _gla_mix(x, g_mix, w_gla, gkup, _row(gla_gk_b[l]), _row(gla_norm[l]))
        x = _merge(x, y_rw, y_s5, y_gla, g_mix, wl[:, o_gate:].astype(BF16),
                   w_branch[l].astype(BF16), w_out[l].astype(BF16))
        kv = _kv_proj(mem, _row(norm_mem), xa_wkv[l].astype(BF16))
        x = _xattn(x, kv, _row(norm_xattn[l]), xa_wq[l].astype(BF16), xa_wo[l].astype(BF16))
        x = _ffn(x, _row(norm_ffn[l]), ffn_w_up[l].astype(BF16), ffn_conv[l], _row(ffn_conv_b[l]),
                 ffn_w_down[l].astype(BF16))
    return _final_norm(x, _row(norm_final))
```

```python
import math

import jax
import jax.numpy as jnp
from jax import lax
from jax.experimental import pallas as pl
from jax.experimental.pallas import tpu as pltpu

F32 = jnp.float32
BF16 = jnp.bfloat16

D_MODEL = 1024
DEPTH = 4
N_BRANCH = 3
MIX_W = 512

RWKV_HEAD = 64
RWKV_HEADS = 8
RWKV_DECAY_RANK = 64
RWKV_A_RANK = 64
RWKV_G_RANK = 128
RWKV_GN_EPS = RWKV_HEAD * 1e-5
RWKV_COLS = 3 * MIX_W + RWKV_DECAY_RANK + RWKV_A_RANK + RWKV_G_RANK

S5_GROUP = 16
S5_GROUPS = 32
S5_STATE = 64
S5_COLS = MIX_W
S5_NS = S5_GROUPS * S5_STATE

GLA_HEADS = 4
GLA_DV = 128
GLA_DK = 64
GLA_GATE_RANK = 16
GLA_GATE_NORM = 16.0
GLA_QK = GLA_HEADS * GLA_DK
GLA_COLS = 2 * GLA_QK + 2 * MIX_W + GLA_GATE_RANK
GLA_COLS_PAD = 1664

XA_HEADS = 4
XA_HEAD = 256
D_FF = 2816
NORM_EPS = 1e-6

LANES = 128
CHUNK = 64
EXP_M05 = math.exp(-0.5)
GELU_C = math.sqrt(2.0 / math.pi)

VMEM_LIMIT = 56 * 1024 * 1024


def _mm(a, b):
    return jnp.dot(a.astype(BF16), b.astype(BF16), preferred_element_type=F32)


def _mm_nt(a, b):
    return lax.dot_general(a.astype(BF16), b.astype(BF16), (((1,), (1,)), ((), ())),
                           preferred_element_type=F32)


def _mm_tn(a, b):
    return lax.dot_general(a.astype(BF16), b.astype(BF16), (((0,), (0,)), ((), ())),
                           preferred_element_type=F32)


def _split(a):
    hi = a.astype(BF16)
    lo = (a - hi.astype(F32)).astype(BF16)
    return hi, lo


def _mm_split_l(a, b_exact):
    hi, lo = _split(a)
    return (jnp.dot(hi, b_exact, preferred_element_type=F32)
            + jnp.dot(lo, b_exact, preferred_element_type=F32))


def _mm_split_r(a_exact, b):
    hi, lo = _split(b)
    return (jnp.dot(a_exact, hi, preferred_element_type=F32)
            + jnp.dot(a_exact, lo, preferred_element_type=F32))


def _sigmoid(x):
    return 1.0 / (1.0 + jnp.exp(-x))


def _rms(x, g):
    ms = jnp.mean(x * x, axis=-1, keepdims=True)
    return x * lax.rsqrt(ms + NORM_EPS) * g


def _iota(shape, dim):
    return lax.broadcasted_iota(jnp.int32, shape, dim)


def _params(sem):
    return pltpu.CompilerParams(dimension_semantics=sem, vmem_limit_bytes=VMEM_LIMIT)


def _full(shape):
    nd = len(shape)
    return pl.BlockSpec(shape, lambda *_: (0,) * nd)


def _norm_mm_kernel(x_ref, g_ref, w_ref, o_ref):
    o_ref[...] = _mm(_rms(x_ref[...], g_ref[...]), w_ref[...]).astype(o_ref.dtype)


def _kv_proj(mem, g, w):
    B, M, D = mem.shape
    N = w.shape[1]
    return pl.pallas_call(
        _norm_mm_kernel,
        grid=(B,),
        in_specs=[pl.BlockSpec((None, M, D), lambda b: (b, 0, 0)),
                  _full((1, D)), _full((D, N))],
        out_specs=pl.BlockSpec((None, M, N), lambda b: (b, 0, 0)),
        out_shape=jax.ShapeDtypeStruct((B, M, N), BF16),
        compiler_params=_params(("parallel",)),
        name="kv_proj",
    )(mem, g, w)


def _s5_kernel(x_ref, g_ref, w_ref, are_ref, aim_ref, bre_ref, bim_ref, cre_ref, cim_ref, d_ref,
               wglu_ref, bglu_ref, o_ref, sre, sim, st_re, st_im):
    nb, tt, D = x_ref.shape
    rows = nb * tt

    @pl.when(pl.program_id(0) == 0)
    def _():
        st_re[...] = jnp.zeros_like(st_re)
        st_im[...] = jnp.zeros_like(st_im)

    p = _mm(_rms(x_ref[...].reshape(rows, D), g_ref[...]), w_ref[...])
    u = pltpu.einshape("btc->tbc", p.reshape(nb, tt, S5_COLS)).reshape(rows, S5_COLS)
    half = S5_NS // 2
    for kt in range(2):
        uk = u[:, 256 * kt:256 * (kt + 1)].astype(BF16)
        sre[:, half * kt:half * (kt + 1)] = jnp.dot(uk, bre_ref[kt], preferred_element_type=F32)
        sim[:, half * kt:half * (kt + 1)] = jnp.dot(uk, bim_ref[kt], preferred_element_type=F32)

    cw = 512
    for cg in range(S5_NS // cw):
        cs = slice(cw * cg, cw * (cg + 1))
        ar = jnp.broadcast_to(are_ref[:, cs], (nb, cw))
        ai = jnp.broadcast_to(aim_ref[:, cs], (nb, cw))

        def body(t, carry):
            re, im = carry
            rs = pl.ds(pl.multiple_of(t * nb, nb), nb)
            nre = ar * re - ai * im + sre[rs, cs]
            nim = ar * im + ai * re + sim[rs, cs]
            sre[rs, cs] = nre
            sim[rs, cs] = nim
            return nre, nim

        re, im = lax.fori_loop(0, tt, body, (st_re[:, cs], st_im[:, cs]))
        st_re[:, cs] = re
        st_im[:, cs] = im

    ys = []
    for jt in range(2):
        hs = slice(half * jt, half * (jt + 1))
        ys.append(_mm(sre[:, hs], cre_ref[jt]) + _mm(sim[:, hs], cim_ref[jt]))
    y = jnp.concatenate(ys, axis=1) + d_ref[...] * u
    y = 0.5 * y * (1.0 + jnp.tanh(GELU_C * (y + 0.044715 * (y * y * y))))
    y = y * _sigmoid(_mm(y, wglu_ref[...]) + bglu_ref[...])
    o_ref[...] = pltpu.einshape("tbc->btc", y.reshape(tt, nb, S5_COLS))


def _s5_mix(x, g, w, are, aim, bre, bim, cre, cim, d, wglu, bglu):
    B, L, D = x.shape
    tt = min(L, 32)
    rows = tt * B
    consts = (g, w, are, aim, bre, bim, cre, cim, d, wglu, bglu)
    return pl.pallas_call(
        _s5_kernel,
        grid=(L // tt,),
        in_specs=[pl.BlockSpec((B, tt, D), lambda i: (0, i, 0))] + [_full(c.shape) for c in consts],
        out_specs=pl.BlockSpec((B, tt, S5_COLS), lambda i: (0, i, 0)),
        out_shape=jax.ShapeDtypeStruct((B, L, S5_COLS), F32),
        scratch_shapes=[pltpu.VMEM((rows, S5_NS), F32), pltpu.VMEM((rows, S5_NS), F32),
                        pltpu.VMEM((B, S5_NS), F32), pltpu.VMEM((B, S5_NS), F32)],
        compiler_params=_params(("arbitrary",)),
        name="s5_mix",
    )(x, *consts)


def _s5_tables(a_re, a_im, b_re, b_im, c_re, c_im, log_step):
    dt = jnp.exp(log_step)[:, None]
    mag = jnp.exp(dt * a_re)
    ang = dt * a_im
    abar_re, abar_im = mag * jnp.cos(ang), mag * jnp.sin(ang)
    den = a_re * a_re + a_im * a_im
    nr, ni = abar_re - 1.0, abar_im
    f_re = ((nr * a_re + ni * a_im) / den)[..., None]
    f_im = ((ni * a_re - nr * a_im) / den)[..., None]
    bbar_re = f_re * b_re - f_im * b_im
    bbar_im = f_re * b_im + f_im * b_re
    gl = S5_GROUPS // 2
    eye = jnp.eye(gl, dtype=a_re.dtype)

    def bmat(bb):
        bb = bb.reshape(2, gl, S5_STATE, S5_GROUP)
        return jnp.einsum('kgnc,gh->kgchn', bb, eye).reshape(2, gl * S5_GROUP, gl * S5_STATE)

    def cmat(cc):
        cc = cc.reshape(2, gl, S5_GROUP, S5_STATE)
        return jnp.einsum('kgcn,gh->kgnhc', cc, eye).reshape(2, gl * S5_STATE, gl * S5_GROUP)

    return (abar_re.reshape(1, S5_NS), abar_im.reshape(1, S5_NS),
            bmat(bbar_re).astype(BF16), bmat(bbar_im).astype(BF16),
            cmat(c_re).astype(BF16), cmat(-c_im).astype(BF16))


def _rwkv_kernel(x_ref, g_ref, w_ref, mu_ref, w0_ref, lora_ref, a0_ref, gup_ref, kk_ref, ka_ref,
                 rk_ref, lnw_ref, lnb_ref, ones_ref, tri_ref, o_ref,
                 prev_ref, st_ref, ld_s, al_s, be_s, k_s, r_s, v_s, y_s, bon_s, g_s, r2_s,
                 mb_s, dl_s, wcol_s):
    tb = x_ref.shape[0]
    C = CHUNK

    @pl.when(pl.program_id(1) == 0)
    def _():
        prev_ref[...] = jnp.zeros_like(prev_ref)
        st_ref[...] = jnp.zeros_like(st_ref)

    p = _mm(_rms(x_ref[...], g_ref[...]), w_ref[...])
    shifted = jnp.where(_iota(p.shape, 0) == 0, prev_ref[0:1, :], pltpu.roll(p, 1, axis=0))
    prev_ref[0:1, :] = p[tb - 1:tb, :]
    p = p + (shifted - p) * mu_ref[...]

    r = p[:, 0:MIX_W]
    k = p[:, MIX_W:2 * MIX_W]
    v = p[:, 2 * MIX_W:3 * MIX_W]
    lin = p[:, 3 * MIX_W:3 * MIX_W + LANES]
    gd = p[:, 3 * MIX_W + LANES:]
    lin = jnp.where(_iota(lin.shape, 1) < RWKV_DECAY_RANK, jnp.tanh(lin), lin)
    proj = _mm(lin, lora_ref[...])
    ld_s[...] = -EXP_M05 * _sigmoid(w0_ref[...] + proj[:, :MIX_W])
    a = _sigmoid(a0_ref[...] + proj[:, MIX_W:])
    g_s[...] = _mm(_sigmoid(gd), gup_ref[...])
    kkv = k * kk_ref[...]
    kk = kkv * lax.rsqrt(_mm(kkv * kkv, ones_ref[...]) + 1e-12)
    k2 = k * (1.0 + (a - 1.0) * ka_ref[...])
    bon_s[...] = _mm(r * k2 * rk_ref[...], ones_ref[...]) * v
    al_s[...] = -kk
    be_s[...] = kk * a
    k_s[...] = k2
    r_s[...] = r
    v_s[...] = v

    m0 = _iota((C, LANES), 1) < RWKV_HEAD
    ri = _iota((2 * C, 2 * C), 0)
    ci = _iota((2 * C, 2 * C), 1)
    same = (ri >= C) == (ci >= C)
    strict = same & (ri > ci)
    incl = same & (ri >= ci)
    incl2 = jnp.concatenate([incl, incl], axis=1)
    eye = (ri == ci).astype(F32)
    bd = (_iota((LANES, LANES), 0) >= RWKV_HEAD) == (_iota((LANES, LANES), 1) >= RWKV_HEAD)
    zc = jnp.zeros((C, LANES), F32)
    z2 = jnp.zeros((2 * C, LANES), BF16)
    n_lvl = C.bit_length() - 1
    n_pair = RWKV_HEADS // 2
    sls = [slice(LANES * j, LANES * (j + 1)) for j in range(n_pair)]

    def decayed(c):
        rs = pl.ds(pl.multiple_of(c * C, C), C)
        ldc = ld_s[rs, :]
        cum = _mm_split_r(tri_ref[...], ldc)
        tot = cum[C - 1:C, :]
        e_neg = jnp.exp(-cum)
        e_st = jnp.exp(tot - cum)
        be = be_s[rs, :]
        kc = k_s[rs, :]
        return dict(rs=rs, c=c, al=al_s[rs, :] * jnp.exp(cum - ldc), bt=be * e_neg, bs=be * e_st,
                    kt=kc * e_neg, ks=kc * e_st, rt=r_s[rs, :] * jnp.exp(cum), vc=v_s[rs, :],
                    wc=jnp.exp(tot))

    def phase_a(i, carry):
        chunks = [decayed(2 * i), decayed(2 * i + 1)]
        items = [(ch, j) for ch in chunks for j in range(n_pair)]
        l4s, gms = [], []
        for ch, j in items:
            sl = sls[j]
            ap, rp = ch["al"][:, sl], ch["rt"][:, sl]
            l4 = jnp.concatenate([jnp.where(m0, ap, 0.0), jnp.where(m0, 0.0, ap),
                                  jnp.where(m0, rp, 0.0), jnp.where(m0, 0.0, rp)], axis=0).astype(BF16)
            r4 = jnp.concatenate([ch["bt"][:, sl], ch["bt"][:, sl], ch["kt"][:, sl], ch["kt"][:, sl]], axis=0)
            l4s.append(l4)
            gms.append(_mm_nt(l4, r4))
        v2s = [jnp.concatenate([ch["vc"][:, sls[j]], ch["vc"][:, sls[j]]], axis=0).astype(BF16)
               for ch, j in items]
        w1s = [_mm(jnp.where(strict, gm[:2 * C, 2 * C:], 0.0), v2) for gm, v2 in zip(gms, v2s)]
        ns = [jnp.where(strict, gm[:2 * C, :2 * C], 0.0) for gm in gms]
        ts = [eye + n for n in ns]
        pws = [n.astype(BF16) for n in ns]
        for _ in range(1, n_lvl):
            pws = [_mm(pw, pw).astype(BF16) for pw in pws]
            ts = [t + _mm(pw, t) for pw, t in zip(pws, ts)]
        xs = [_mm(t, jnp.concatenate([w1.astype(BF16), l4[:2 * C]], axis=1))
              for t, w1, l4 in zip(ts, w1s, l4s)]
        uls = [x[:, :LANES].astype(BF16) for x in xs]
        a2s = [x[:, LANES:].astype(BF16) for x in xs]
        yl2s = [_mm(jnp.where(incl2, gm[2 * C:, :], 0.0), jnp.concatenate([ul, v2], axis=0))
                for gm, ul, v2 in zip(gms, uls, v2s)]
        ras = [_mm(jnp.where(incl, gm[2 * C:, :2 * C], 0.0), a2) for gm, a2 in zip(gms, a2s)]
        for (ch, j), yl2, ra in zip(items, yl2s, ras):
            y_s[ch["rs"], sls[j]] = jnp.where(m0, yl2[:C], yl2[C:])
            r2_s[ch["rs"], sls[j]] = ra[:C] + ra[C:] + ch["rt"][:, sls[j]]
        for (ch, j), ul, a2 in zip(items, uls, a2s):
            sl = sls[j]
            lb = jnp.concatenate([jnp.where(m0, ch["bs"][:, sl], 0.0), jnp.where(m0, 0.0, ch["bs"][:, sl]),
                                  ch["ks"][:, sl], zc], axis=0)
            rb = jnp.concatenate([jnp.concatenate([a2, z2], axis=0),
                                  jnp.concatenate([ul, ch["vc"][:, sl].astype(BF16), zc.astype(BF16)], axis=0)],
                                 axis=1)
            md = _mm_tn(lb, rb)
            idx = ch["c"] * n_pair + j
            mb_s[idx] = md[:, :LANES].astype(BF16)
            dl_s[idx] = jnp.where(bd, md[:, LANES:], 0.0)
            wcol_s[idx] = jnp.broadcast_to(ch["wc"][:, sl], (LANES, LANES)).T
        return carry

    lax.fori_loop(0, tb // (2 * C), phase_a, 0)

    def phase_b(c, carry):
        rs = pl.ds(pl.multiple_of(c * C, C), C)
        sts = [st_ref[j] for j in range(n_pair)]
        stb = [st.astype(BF16) for st in sts]
        upd = [_mm(mb_s[c * n_pair + j], stb[j]) for j in range(n_pair)]
        for j in range(n_pair):
            y_s[rs, sls[j]] += _mm(r2_s[rs, sls[j]], stb[j])
        for j in range(n_pair):
            idx = c * n_pair + j
            st_ref[j] = wcol_s[idx] * sts[j] + upd[j] + dl_s[idx]
        return carry

    lax.fori_loop(0, tb // C, phase_b, 0)

    y = y_s[...]
    inv = 1.0 / RWKV_HEAD
    mean = _mm(y, ones_ref[...]) * inv
    dlt = y - mean
    var = _mm(dlt * dlt, ones_ref[...]) * inv
    yn = dlt * lax.rsqrt(var + RWKV_GN_EPS) * lnw_ref[...] + lnb_ref[...]
    o_ref[...] = (yn + bon_s[...]) * g_s[...]


def _rwkv_mix(x, g, w, mu, w0, lora, a0, gup, kk, ka, rk, lnw, lnb):
    B, L, D = x.shape
    tb = min(L, 512)
    ones = jnp.kron(jnp.eye(RWKV_HEADS, dtype=F32), jnp.ones((RWKV_HEAD, RWKV_HEAD), F32)).astype(BF16)
    tri = jnp.tril(jnp.ones((CHUNK, CHUNK), F32)).astype(BF16)
    consts = (g, w, mu, w0, lora, a0, gup, kk, ka, rk, lnw, lnb, ones, tri)
    act = lambda: pltpu.VMEM((tb, MIX_W), F32)
    n_mat = (tb // CHUNK) * (RWKV_HEADS // 2)
    mat = lambda dt: pltpu.VMEM((n_mat, LANES, LANES), dt)
    return pl.pallas_call(
        _rwkv_kernel,
        grid=(B, L // tb),
        in_specs=[pl.BlockSpec((None, tb, D), lambda b, i: (b, i, 0))] + [_full(c.shape) for c in consts],
        out_specs=pl.BlockSpec((None, tb, MIX_W), lambda b, i: (b, i, 0)),
        out_shape=jax.ShapeDtypeStruct((B, L, MIX_W), F32),
        scratch_shapes=[pltpu.VMEM((8, RWKV_COLS), F32),
                        pltpu.VMEM((RWKV_HEADS // 2, LANES, LANES), F32)] + [act() for _ in range(10)]
        + [mat(BF16), mat(F32), mat(F32)],
        compiler_params=_params(("parallel", "arbitrary")),
        name="rwkv_mix",
    )(x, *consts)


def _gla_kernel(x_ref, g_ref, w_ref, gkup_ref, gkb_ref, ng_ref, tri_ref, o_ref,
                st_ref, q_s, k_s, v_s, gk_s, o_s):
    tb = x_ref.shape[0]
    C = CHUNK

    @pl.when(pl.program_id(1) == 0)
    def _():
        st_ref[...] = jnp.zeros_like(st_ref)

    p = _mm(_rms(x_ref[...], g_ref[...]), w_ref[...])
    q_s[...] = p[:, 0:GLA_QK] * (GLA_DK ** -0.5)
    k_s[...] = p[:, GLA_QK:2 * GLA_QK]
    v_s[...] = p[:, 2 * GLA_QK:2 * GLA_QK + MIX_W]
    go = p[:, 2 * GLA_QK + MIX_W:2 * GLA_QK + 2 * MIX_W]
    z = _mm(p[:, 2 * GLA_QK + 2 * MIX_W:], gkup_ref[...]) + gkb_ref[...]
    gk_s[...] = (jnp.minimum(z, 0.0) - jnp.log(1.0 + jnp.exp(-jnp.abs(z)))) * (1.0 / GLA_GATE_NORM)

    m0 = _iota((C, LANES), 1) < GLA_DK
    mv = _iota((C, 2 * GLA_DV), 1) < GLA_DV
    low = (_iota((2 * C, C), 0) % C) >= _iota((2 * C, C), 1)
    bd = (_iota((LANES, 2 * GLA_DV), 0) >= GLA_DK) == (_iota((LANES, 2 * GLA_DV), 1) >= GLA_DV)

    n_pair = GLA_HEADS // 2
    sls = [slice(LANES * j, LANES * (j + 1)) for j in range(n_pair)]

    def decayed(c):
        rs = pl.ds(pl.multiple_of(c * C, C), C)
        bcum = _mm_split_r(tri_ref[...], gk_s[rs, :])
        blast = bcum[C - 1:C, :]
        kc = k_s[rs, :]
        return dict(rs=rs, qd=q_s[rs, :] * jnp.exp(bcum), ki=kc * jnp.exp(-bcum),
                    ks=kc * jnp.exp(blast - bcum), el=jnp.exp(blast), vc=v_s[rs, :])

    def chunk_pair(i, carry):
        chunks = [decayed(2 * i), decayed(2 * i + 1)]
        items = [(ch, j) for ch in chunks for j in range(n_pair)]
        vps = [ch["vc"][:, 2 * GLA_DV * j:2 * GLA_DV * (j + 1)].astype(BF16) for ch, j in items]
        atts = []
        for ch, j in items:
            qp = ch["qd"][:, sls[j]]
            l2 = jnp.concatenate([jnp.where(m0, qp, 0.0), jnp.where(m0, 0.0, qp)], axis=0)
            atts.append(jnp.where(low, _mm_nt(l2, ch["ki"][:, sls[j]]), 0.0))
        dss = [jnp.where(bd, _mm_tn(ch["ks"][:, sls[j]], vp), 0.0) for (ch, j), vp in zip(items, vps)]
        o2s = [_mm(att, vp) for att, vp in zip(atts, vps)]
        sts = [st_ref[j] for j in range(n_pair)]
        for n, (ch, j) in enumerate(items):
            sv = slice(2 * GLA_DV * j, 2 * GLA_DV * (j + 1))
            o_s[ch["rs"], sv] = jnp.where(mv, o2s[n][:C], o2s[n][C:]) + _mm(ch["qd"][:, sls[j]], sts[j])
            ecol = jnp.broadcast_to(ch["el"][:, sls[j]], (LANES, LANES)).T
            sts[j] = jnp.concatenate([ecol, ecol], axis=1) * sts[j] + dss[n]
        for j in range(n_pair):
            st_ref[j] = sts[j]
        return carry

    lax.fori_loop(0, tb // (2 * C), chunk_pair, 0)

    o = o_s[...]
    outs = []
    for h in range(GLA_HEADS):
        oh = o[:, GLA_DV * h:GLA_DV * (h + 1)]
        outs.append(oh * lax.rsqrt(jnp.mean(oh * oh, axis=-1, keepdims=True) + 1e-5) * ng_ref[...])
    o_ref[...] = jnp.concatenate(outs, axis=1) * (go * _sigmoid(go))


def _gla_mix(x, g, w, gkup, gkb, ng):
    B, L, D = x.shape
    tb = min(L, 512)
    tri = jnp.tril(jnp.ones((CHUNK, CHUNK), F32)).astype(BF16)
    consts = (g, w, gkup, gkb, ng, tri)
    return pl.pallas_call(
        _gla_kernel,
        grid=(B, L // tb),
        in_specs=[pl.BlockSpec((None, tb, D), lambda b, i: (b, i, 0))] + [_full(c.shape) for c in consts],
        out_specs=pl.BlockSpec((None, tb, MIX_W), lambda b, i: (b, i, 0)),
        out_shape=jax.ShapeDtypeStruct((B, L, MIX_W), F32),
        scratch_shapes=[pltpu.VMEM((GLA_HEADS // 2, LANES, 2 * GLA_DV), F32),
                        pltpu.VMEM((tb, GLA_QK), F32), pltpu.VMEM((tb, GLA_QK), F32),
                        pltpu.VMEM((tb, MIX_W), F32), pltpu.VMEM((tb, GLA_QK), F32),
                        pltpu.VMEM((tb, MIX_W), F32)],
        compiler_params=_params(("parallel", "arbitrary")),
        name="gla_mix",
    )(x, *consts)


def _merge_kernel(x_ref, yrw_ref, ys5_ref, ygla_ref, g_ref, wg_ref, wb_ref, wo_ref, o_ref):
    x = x_ref[...]
    gates = _sigmoid(_mm(_rms(x, g_ref[...]), wg_ref[...]))
    merged = (gates[:, 0:D_MODEL] * _mm(yrw_ref[...], wb_ref[0])
              + gates[:, D_MODEL:2 * D_MODEL] * _mm(ys5_ref[...], wb_ref[1])
              + gates[:, 2 * D_MODEL:] * _mm(ygla_ref[...], wb_ref[2]))
    o_ref[...] = x + _mm(merged, wo_ref[...])


def _merge(x, y_rw, y_s5, y_gla, g, wg, wb, wo):
    B, L, D = x.shape
    tm = min(L, 512)
    bspec = lambda w: pl.BlockSpec((None, tm, w), lambda b, i: (b, i, 0))
    return pl.pallas_call(
        _merge_kernel,
        grid=(B, L // tm),
        in_specs=[bspec(D), bspec(MIX_W), bspec(MIX_W), bspec(MIX_W),
                  _full(g.shape), _full(wg.shape), _full(wb.shape), _full(wo.shape)],
        out_specs=bspec(D),
        out_shape=jax.ShapeDtypeStruct((B, L, D), F32),
        compiler_params=_params(("parallel", "parallel")),
        name="merge",
    )(x, y_rw, y_s5, y_gla, g, wg, wb, wo)


def _xattn_kernel(x_ref, kv_ref, g_ref, wq_ref, wo_ref, o_ref):
    x = x_ref[...]
    q = _mm(_rms(x, g_ref[...]), wq_ref[...])
    outs = []
    for h in range(XA_HEADS):
        kh = kv_ref[:, XA_HEAD * h:XA_HEAD * (h + 1)]
        vh = kv_ref[:, D_MODEL + XA_HEAD * h:D_MODEL + XA_HEAD * (h + 1)]
        s = _mm_nt(q[:, XA_HEAD * h:XA_HEAD * (h + 1)], kh) * (XA_HEAD ** -0.5)
        e = jnp.exp(s - jnp.max(s, axis=-1, keepdims=True))
        outs.append(_mm(e / jnp.sum(e, axis=-1, keepdims=True), vh))
    o_ref[...] = x + _mm(jnp.concatenate(outs, axis=1), wo_ref[...])


def _xattn(x, kv, g, wq, wo):
    B, L, D = x.shape
    M = kv.shape[1]
    tq = min(L, 512)
    xspec = pl.BlockSpec((None, tq, D), lambda b, i: (b, i, 0))
    return pl.pallas_call(
        _xattn_kernel,
        grid=(B, L // tq),
        in_specs=[xspec, pl.BlockSpec((None, M, 2 * D), lambda b, i: (b, 0, 0)),
                  _full(g.shape), _full(wq.shape), _full(wo.shape)],
        out_specs=xspec,
        out_shape=jax.ShapeDtypeStruct((B, L, D), F32),
        compiler_params=_params(("parallel", "parallel")),
        name="xattn",
    )(x, kv, g, wq, wo)


FFN_HALO = 16


def _ffn_kernel(x_ref, xh_ref, g_ref, wug_ref, wuv_ref, cw_ref, cb_ref, wd_ref, o_ref,
                hn_s, ug_a, uv_a, ug_b, uv_b):
    nj = wug_ref.shape[0]
    halo = _rms(xh_ref[...], g_ref[...])
    halo = jnp.where(pl.program_id(1) == 0, 0.0, halo)
    hn_s[0:FFN_HALO, :] = halo.astype(BF16)
    hn_s[FFN_HALO:, :] = _rms(x_ref[...], g_ref[...]).astype(BF16)
    o_ref[...] = x_ref[...]

    def up(jb, wr_g, wr_v):
        hn = hn_s[...]
        wr_g[...] = jnp.dot(hn, wug_ref[jb], preferred_element_type=F32)
        wr_v[...] = jnp.dot(hn, wuv_ref[jb], preferred_element_type=F32)

    def conv(u, jb):
        cw = cw_ref[jb]
        c = (cw[2:3, :] * u + cw[1:2, :] * pltpu.roll(u, 1, axis=0)
             + cw[0:1, :] * pltpu.roll(u, 2, axis=0) + cb_ref[jb])
        return c[FFN_HALO:, :]

    def down(jb, rd_g, rd_v):
        gate = conv(rd_g[...], jb)
        act = gate * _sigmoid(gate) * conv(rd_v[...], nj + jb)
        o_ref[...] += _mm(act, wd_ref[jb])

    up(0, ug_a, uv_a)

    def two_blocks(t, carry):
        up(2 * t + 1, ug_b, uv_b)
        down(2 * t, ug_a, uv_a)
        up(2 * t + 2, ug_a, uv_a)
        down(2 * t + 1, ug_b, uv_b)
        return carry

    lax.fori_loop(0, (nj - 1) // 2, two_blocks, 0)
    down(nj - 1, ug_a, uv_a)


def _ffn(x, g, w_up, conv_w, conv_b, w_down):
    B, L, D = x.shape
    tm = min(L, 1024)
    bn = 256
    nj = D_FF // bn
    assert nj % 2 == 1 and nj * bn == D_FF
    hb = tm // FFN_HALO
    wu = w_up.reshape(D, 2 * nj, bn).transpose(1, 0, 2)
    cw = conv_w.reshape(3, 2 * nj, bn).transpose(1, 0, 2)
    cb = conv_b.reshape(2 * nj, 1, bn)
    wd = w_down.reshape(nj, bn, D)
    xspec = pl.BlockSpec((None, tm, D), lambda b, i: (b, i, 0))
    resident = lambda a: pl.BlockSpec(a.shape, lambda b, i: (0,) * a.ndim, pipeline_mode=pl.Buffered(1))
    ubuf = lambda: pltpu.VMEM((FFN_HALO + tm, bn), F32)
    return pl.pallas_call(
        _ffn_kernel,
        grid=(B, L // tm),
        in_specs=[xspec,
                  pl.BlockSpec((None, FFN_HALO, D), lambda b, i: (b, jnp.maximum(i * hb - 1, 0), 0)),
                  _full(g.shape),
                  pl.BlockSpec((nj, D, bn), lambda b, i: (0, 0, 0), pipeline_mode=pl.Buffered(1)),
                  pl.BlockSpec((nj, D, bn), lambda b, i: (1, 0, 0), pipeline_mode=pl.Buffered(1)),
                  resident(cw), resident(cb), resident(wd)],
        out_specs=xspec,
        out_shape=jax.ShapeDtypeStruct((B, L, D), F32),
        scratch_shapes=[pltpu.VMEM((FFN_HALO + tm, D), BF16), ubuf(), ubuf(), ubuf(), ubuf()],
        compiler_params=_params(("parallel", "parallel")),
        name="conv_ffn",
    )(x, x, g, wu, wu, cw, cb, wd)


def _final_norm_kernel(x_ref, g_ref, o_ref):
    o_ref[...] = _rms(x_ref[...], g_ref[...])


def _final_norm(x, g):
    B, L, D = x.shape
    tm = min(L, 1024)
    spec = pl.BlockSpec((None, tm, D), lambda b, i: (b, i, 0))
    return pl.pallas_call(
        _final_norm_kernel,
        grid=(B, L // tm),
        in_specs=[spec, _full(g.shape)],
        out_specs=spec,
        out_shape=jax.ShapeDtypeStruct((B, L, D), F32),
        compiler_params=_params(("parallel", "parallel")),
        name="final_norm",
    )(x, g)


def _row(v):
    return v.reshape(1, -1)


def kernel(x, mem, norm_mix, w_in, rw_mu, rw_w0, rw_w_up, rw_a0, rw_a_up, rw_g_up, rw_k_k, rw_k_a, rw_r_k, rw_ln_w, rw_ln_b, s5_a_re, s5_a_im, s5_b_re, s5_b_im, s5_c_re, s5_c_im, s5_d, s5_log_step, s5_w_glu, s5_b_glu, gla_gk_up, gla_gk_b, gla_norm, w_branch, w_out, norm_mem, norm_xattn, xa_wq, xa_wkv, xa_wo, norm_ffn, ffn_w_up, ffn_conv, ffn_conv_b, ffn_w_down, norm_final):
    B, L, D = x.shape
    o_s5 = RWKV_COLS
    o_gla = o_s5 + S5_COLS
    o_gate = o_gla + GLA_COLS
    for l in range(DEPTH):
        wl = w_in[l]
        g_mix = _row(norm_mix[l])
        tabs = _s5_tables(s5_a_re[l], s5_a_im[l], s5_b_re[l], s5_b_im[l], s5_c_re[l], s5_c_im[l],
                          s5_log_step[l])
        y_s5 = _s5_mix(x, g_mix, wl[:, o_s5:o_gla].astype(BF16), *tabs, _row(s5_d[l]),
                       s5_w_glu[l].astype(BF16), _row(s5_b_glu[l]))
        zeros = jnp.zeros((RWKV_DECAY_RANK, MIX_W), w_in.dtype)
        lora = jnp.concatenate([jnp.concatenate([rw_w_up[l], zeros], axis=1),
                                jnp.concatenate([zeros, rw_a_up[l]], axis=1)], axis=0).astype(BF16)
        y_rw = _rwkv_mix(x, g_mix, wl[:, :o_s5].astype(BF16), _row(rw_mu[l]), _row(rw_w0[l]), lora,
                         _row(rw_a0[l]), rw_g_up[l].astype(BF16), _row(rw_k_k[l]), _row(rw_k_a[l]),
                         _row(rw_r_k[l]), _row(rw_ln_w[l]), _row(rw_ln_b[l]))
        w_gla = wl[:, o_gla:o_gate]
        w_gla = jnp.concatenate(
            [w_gla[:, :2 * GLA_QK + MIX_W], w_gla[:, 2 * GLA_QK + MIX_W + GLA_GATE_RANK:],
             w_gla[:, 2 * GLA_QK + MIX_W:2 * GLA_QK + MIX_W + GLA_GATE_RANK],
             jnp.zeros((D, GLA_COLS_PAD - GLA_COLS), w_in.dtype)], axis=1).astype(BF16)
        gkup = jnp.concatenate([gla_gk_up[l], jnp.zeros((LANES - GLA_GATE_RANK, GLA_QK), w_in.dtype)],
                               axis=0).astype(BF16)
        y_gla = _gla_mix(x, g_mix, w_gla, gkup, _row(gla_gk_b[l]), _row(gla_norm[l]))
        x = _merge(x, y_rw, y_s5, y_gla, g_mix, wl[:, o_gate:].astype(BF16),
                   w_branch[l].astype(BF16), w_out[l].astype(BF16))
        kv = _kv_proj(mem, _row(norm_mem), xa_wkv[l].astype(BF16))
        x = _xattn(x, kv, _row(norm_xattn[l]), xa_wq[l].astype(BF16), xa_wo[l].astype(BF16))
        x = _ffn(x, _row(norm_ffn[l]), ffn_w_up[l].astype(BF16), ffn_conv[l], _row(ffn_conv_b[l]),
                 ffn_w_down[l].astype(BF16))
    return _final_norm(x, _row(norm_final))
```

```python
import functools
import math

import jax
import jax.numpy as jnp
from jax import lax
from jax.experimental import pallas as pl
from jax.experimental.pallas import tpu as pltpu

F32 = jnp.float32
BF16 = jnp.bfloat16

D_MODEL = 1024
DEPTH = 4
N_BRANCH = 3
MIX_W = 512

RWKV_HEAD = 64
RWKV_HEADS = 8
RWKV_DECAY_RANK = 64
RWKV_A_RANK = 64
RWKV_G_RANK = 128
RWKV_GN_EPS = RWKV_HEAD * 1e-5
RWKV_COLS = 3 * MIX_W + RWKV_DECAY_RANK + RWKV_A_RANK + RWKV_G_RANK

S5_GROUP = 16
S5_GROUPS = 32
S5_STATE = 64
S5_COLS = MIX_W
S5_NS = S5_GROUPS * S5_STATE

GLA_HEADS = 4
GLA_DV = 128
GLA_DK = 64
GLA_GATE_RANK = 16
GLA_GATE_NORM = 16.0
GLA_QK = GLA_HEADS * GLA_DK
GLA_COLS = 2 * GLA_QK + 2 * MIX_W + GLA_GATE_RANK
GLA_COLS_PAD = 1664

XA_HEADS = 4
XA_HEAD = 256
D_FF = 2816
NORM_EPS = 1e-6

LANES = 128
CHUNK = 64
RWKV_TRIP = 4
GLA_TRIP = 4
EXP_M05 = math.exp(-0.5)
GELU_C = math.sqrt(2.0 / math.pi)

VMEM_LIMIT = 56 * 1024 * 1024


def _mm(a, b):
    return jnp.dot(a.astype(BF16), b.astype(BF16), preferred_element_type=F32)


def _mm_nt(a, b):
    return lax.dot_general(a.astype(BF16), b.astype(BF16), (((1,), (1,)), ((), ())),
                           preferred_element_type=F32)


def _mm_tn(a, b):
    return lax.dot_general(a.astype(BF16), b.astype(BF16), (((0,), (0,)), ((), ())),
                           preferred_element_type=F32)


def _split(a):
    hi = a.astype(BF16)
    lo = (a - hi.astype(F32)).astype(BF16)
    return hi, lo


def _mm_split_l(a, b_exact):
    hi, lo = _split(a)
    return (jnp.dot(hi, b_exact, preferred_element_type=F32)
            + jnp.dot(lo, b_exact, preferred_element_type=F32))


def _mm_split_r(a_exact, b):
    hi, lo = _split(b)
    return (jnp.dot(a_exact, hi, preferred_element_type=F32)
            + jnp.dot(a_exact, lo, preferred_element_type=F32))


def _sigmoid(x):
    return 1.0 / (1.0 + jnp.exp(-x))


def _rms(x, g):
    ms = jnp.mean(x * x, axis=-1, keepdims=True)
    return x * lax.rsqrt(ms + NORM_EPS) * g


def _iota(shape, dim):
    return lax.broadcasted_iota(jnp.int32, shape, dim)


def _params(sem):
    return pltpu.CompilerParams(dimension_semantics=sem, vmem_limit_bytes=VMEM_LIMIT)


def _full(shape):
    nd = len(shape)
    return pl.BlockSpec(shape, lambda *_: (0,) * nd)


def _norm_mm_kernel(x_ref, g_ref, w_ref, o_ref):
    o_ref[...] = _mm(_rms(x_ref[...], g_ref[...]), w_ref[...]).astype(o_ref.dtype)


def _kv_proj(mem, g, w):
    B, M, D = mem.shape
    N = w.shape[1]
    return pl.pallas_call(
        _norm_mm_kernel,
        grid=(B,),
        in_specs=[pl.BlockSpec((None, M, D), lambda b: (b, 0, 0)),
                  _full((1, D)), _full((D, N))],
        out_specs=pl.BlockSpec((None, M, N), lambda b: (b, 0, 0)),
        out_shape=jax.ShapeDtypeStruct((B, M, N), BF16),
        compiler_params=_params(("parallel",)),
        name="kv_proj",
    )(mem, g, w)


def _s5_kernel(x_ref, g_ref, w_ref, are_ref, aim_ref, bre_ref, bim_ref, cre_ref, cim_ref, d_ref,
               wglu_ref, bglu_ref, o_ref, sre, sim, st_re, st_im):
    nb, tt, D = x_ref.shape
    rows = nb * tt

    @pl.when(pl.program_id(0) == 0)
    def _():
        st_re[...] = jnp.zeros_like(st_re)
        st_im[...] = jnp.zeros_like(st_im)

    p = _mm(_rms(x_ref[...].reshape(rows, D), g_ref[...]), w_ref[...])
    u = pltpu.einshape("btc->tbc", p.reshape(nb, tt, S5_COLS)).reshape(rows, S5_COLS)
    half = S5_NS // 2
    for kt in range(2):
        uk = u[:, 256 * kt:256 * (kt + 1)].astype(BF16)
        sre[:, half * kt:half * (kt + 1)] = jnp.dot(uk, bre_ref[kt], preferred_element_type=F32)
        sim[:, half * kt:half * (kt + 1)] = jnp.dot(uk, bim_ref[kt], preferred_element_type=F32)

    cw = 512
    for cg in range(S5_NS // cw):
        cs = slice(cw * cg, cw * (cg + 1))
        ar = jnp.broadcast_to(are_ref[:, cs], (nb, cw))
        ai = jnp.broadcast_to(aim_ref[:, cs], (nb, cw))

        def body(t, carry):
            re, im = carry
            rs = pl.ds(pl.multiple_of(t * nb, nb), nb)
            nre = ar * re - ai * im + sre[rs, cs]
            nim = ar * im + ai * re + sim[rs, cs]
            sre[rs, cs] = nre
            sim[rs, cs] = nim
            return nre, nim

        re, im = lax.fori_loop(0, tt, body, (st_re[:, cs], st_im[:, cs]))
        st_re[:, cs] = re
        st_im[:, cs] = im

    ys = []
    for jt in range(2):
        hs = slice(half * jt, half * (jt + 1))
        ys.append(_mm(sre[:, hs], cre_ref[jt]) + _mm(sim[:, hs], cim_ref[jt]))
    y = jnp.concatenate(ys, axis=1) + d_ref[...] * u
    y = 0.5 * y * (1.0 + jnp.tanh(GELU_C * (y + 0.044715 * (y * y * y))))
    y = y * _sigmoid(_mm(y, wglu_ref[...]) + bglu_ref[...])
    o_ref[...] = pltpu.einshape("tbc->btc", y.reshape(tt, nb, S5_COLS))


def _s5_mix(x, g, w, are, aim, bre, bim, cre, cim, d, wglu, bglu):
    B, L, D = x.shape
    tt = min(L, 32)
    rows = tt * B
    consts = (g, w, are, aim, bre, bim, cre, cim, d, wglu, bglu)
    return pl.pallas_call(
        _s5_kernel,
        grid=(L // tt,),
        in_specs=[pl.BlockSpec((B, tt, D), lambda i: (0, i, 0))] + [_full(c.shape) for c in consts],
        out_specs=pl.BlockSpec((B, tt, S5_COLS), lambda i: (0, i, 0)),
        out_shape=jax.ShapeDtypeStruct((B, L, S5_COLS), F32),
        scratch_shapes=[pltpu.VMEM((rows, S5_NS), F32), pltpu.VMEM((rows, S5_NS), F32),
                        pltpu.VMEM((B, S5_NS), F32), pltpu.VMEM((B, S5_NS), F32)],
        compiler_params=_params(("arbitrary",)),
        name="s5_mix",
    )(x, *consts)


def _s5_tables(a_re, a_im, b_re, b_im, c_re, c_im, log_step):
    dt = jnp.exp(log_step)[:, None]
    mag = jnp.exp(dt * a_re)
    ang = dt * a_im
    abar_re, abar_im = mag * jnp.cos(ang), mag * jnp.sin(ang)
    den = a_re * a_re + a_im * a_im
    nr, ni = abar_re - 1.0, abar_im
    f_re = ((nr * a_re + ni * a_im) / den)[..., None]
    f_im = ((ni * a_re - nr * a_im) / den)[..., None]
    bbar_re = f_re * b_re - f_im * b_im
    bbar_im = f_re * b_im + f_im * b_re
    gl = S5_GROUPS // 2
    eye = jnp.eye(gl, dtype=a_re.dtype)

    def bmat(bb):
        bb = bb.reshape(2, gl, S5_STATE, S5_GROUP)
        return jnp.einsum('kgnc,gh->kgchn', bb, eye).reshape(2, gl * S5_GROUP, gl * S5_STATE)

    def cmat(cc):
        cc = cc.reshape(2, gl, S5_GROUP, S5_STATE)
        return jnp.einsum('kgcn,gh->kgnhc', cc, eye).reshape(2, gl * S5_STATE, gl * S5_GROUP)

    return (abar_re.reshape(1, S5_NS), abar_im.reshape(1, S5_NS),
            bmat(bbar_re).astype(BF16), bmat(bbar_im).astype(BF16),
            cmat(c_re).astype(BF16), cmat(-c_im).astype(BF16))


def _rwkv_kernel(x_ref, g_ref, w_ref, mu_ref, w0_ref, lora_ref, a0_ref, gup_ref, kk_ref, ka_ref,
                 rk_ref, lnw_ref, lnb_ref, ones_ref, tri_ref, o_ref,
                 prev_ref, st_ref, ld_s, al_s, be_s, k_s, r_s, v_s, y_s, bon_s, g_s, r2_s,
                 mb_s, dl_s, wcol_s):
    tb = x_ref.shape[0]
    C = CHUNK

    @pl.when(pl.program_id(1) == 0)
    def _():
        prev_ref[...] = jnp.zeros_like(prev_ref)
        st_ref[...] = jnp.zeros_like(st_ref)

    p = _mm(_rms(x_ref[...], g_ref[...]), w_ref[...])
    shifted = jnp.where(_iota(p.shape, 0) == 0, prev_ref[0:1, :], pltpu.roll(p, 1, axis=0))
    prev_ref[0:1, :] = p[tb - 1:tb, :]
    p = p + (shifted - p) * mu_ref[...]

    r = p[:, 0:MIX_W]
    k = p[:, MIX_W:2 * MIX_W]
    v = p[:, 2 * MIX_W:3 * MIX_W]
    lin = p[:, 3 * MIX_W:3 * MIX_W + LANES]
    gd = p[:, 3 * MIX_W + LANES:]
    lin = jnp.where(_iota(lin.shape, 1) < RWKV_DECAY_RANK, jnp.tanh(lin), lin)
    proj = _mm(lin, lora_ref[...])
    ld_s[...] = -EXP_M05 * _sigmoid(w0_ref[...] + proj[:, :MIX_W])
    a = _sigmoid(a0_ref[...] + proj[:, MIX_W:])
    g_s[...] = _mm(_sigmoid(gd), gup_ref[...])
    kkv = k * kk_ref[...]
    def head_sum(t):
        return jnp.concatenate([_mm(t[:, LANES * i:LANES * (i + 1)], ones_ref[...])
                                for i in range(MIX_W // LANES)], axis=1)

    kk = kkv * lax.rsqrt(head_sum(kkv * kkv) + 1e-12)
    k2 = k * (1.0 + (a - 1.0) * ka_ref[...])
    bon_s[...] = head_sum(r * k2 * rk_ref[...]) * v
    al_s[...] = -kk
    be_s[...] = kk * a
    k_s[...] = k2
    r_s[...] = r
    v_s[...] = v

    m0 = _iota((C, LANES), 1) < RWKV_HEAD
    ri = _iota((2 * C, 2 * C), 0)
    ci = _iota((2 * C, 2 * C), 1)
    same = (ri >= C) == (ci >= C)
    strict = same & (ri > ci)
    incl = same & (ri >= ci)
    top = ri < C
    eye = (ri == ci).astype(F32)
    bd = (_iota((LANES, LANES), 0) >= RWKV_HEAD) == (_iota((LANES, LANES), 1) >= RWKV_HEAD)
    zc = jnp.zeros((C, LANES), F32)
    z2 = jnp.zeros((2 * C, LANES), BF16)
    n_lvl = C.bit_length() - 1
    n_pair = RWKV_HEADS // 2
    sls = [slice(LANES * j, LANES * (j + 1)) for j in range(n_pair)]

    def decayed(c):
        rs = pl.ds(pl.multiple_of(c * C, C), C)
        ldc = ld_s[rs, :]
        cum = _mm_split_r(tri_ref[...], ldc)
        tot = cum[C - 1:C, :]
        e_neg = jnp.exp(-cum)
        e_st = jnp.exp(tot - cum)
        be = be_s[rs, :]
        kc = k_s[rs, :]
        return dict(rs=rs, c=c, al=al_s[rs, :] * jnp.exp(cum - ldc), bt=be * e_neg, bs=be * e_st,
                    kt=kc * e_neg, ks=kc * e_st, rt=r_s[rs, :] * jnp.exp(cum), vc=v_s[rs, :],
                    wc=jnp.exp(tot))

    def phase_a(i, carry):
        chunks = [decayed(RWKV_TRIP * i + q) for q in range(RWKV_TRIP)]
        items = [(ch, j) for ch in chunks for j in range(n_pair)]
        l4s, gms = [], []
        for ch, j in items:
            sl = sls[j]
            ap, rp = ch["al"][:, sl], ch["rt"][:, sl]
            l4 = jnp.concatenate([jnp.where(m0, ap, 0.0), jnp.where(m0, 0.0, ap),
                                  jnp.where(m0, rp, 0.0), jnp.where(m0, 0.0, rp)], axis=0).astype(BF16)
            r2 = jnp.concatenate([ch["bt"][:, sl], ch["kt"][:, sl]], axis=0)
            l4s.append(l4)
            gms.append(_mm_nt(l4, r2))
        gas = [gm[:2 * C] for gm in gms]
        grs = [gm[2 * C:] for gm in gms]
        gas_sw = [pltpu.roll(ga, C, axis=1) for ga in gas]
        grs_sw = [pltpu.roll(gr, C, axis=1) for gr in grs]
        v2s = [jnp.concatenate([ch["vc"][:, sls[j]], ch["vc"][:, sls[j]]], axis=0).astype(BF16)
               for ch, j in items]
        w1s = [_mm(jnp.where(strict, jnp.where(top, sw, ga), 0.0), v2) for ga, sw, v2 in zip(gas, gas_sw, v2s)]
        ns = [jnp.where(strict, jnp.where(top, ga, sw), 0.0) for ga, sw in zip(gas, gas_sw)]
        ts = [eye + n for n in ns]
        pws = [n.astype(BF16) for n in ns]
        for _ in range(1, n_lvl):
            pws = [_mm(pw, pw).astype(BF16) for pw in pws]
            ts = [t + _mm(pw, t) for pw, t in zip(pws, ts)]
        xs = [_mm(t, jnp.concatenate([w1.astype(BF16), l4[:2 * C]], axis=1))
              for t, w1, l4 in zip(ts, w1s, l4s)]
        uls = [x[:, :LANES].astype(BF16) for x in xs]
        a2s = [x[:, LANES:].astype(BF16) for x in xs]
        arbs = [jnp.where(incl, jnp.where(top, gr, sw), 0.0).astype(BF16) for gr, sw in zip(grs, grs_sw)]
        arks = [jnp.where(incl, jnp.where(top, sw, gr), 0.0).astype(BF16) for gr, sw in zip(grs, grs_sw)]
        yl2s = [_mm(jnp.concatenate([arb, ark], axis=1), jnp.concatenate([ul, v2], axis=0))
                for arb, ark, ul, v2 in zip(arbs, arks, uls, v2s)]
        ras = [_mm(arb, a2) for arb, a2 in zip(arbs, a2s)]
        for (ch, j), yl2, ra in zip(items, yl2s, ras):
            y_s[ch["rs"], sls[j]] = jnp.where(m0, yl2[:C], yl2[C:])
            r2_s[ch["rs"], sls[j]] = ra[:C] + ra[C:] + ch["rt"][:, sls[j]]
        for (ch, j), ul, a2 in zip(items, uls, a2s):
            sl = sls[j]
            lb = jnp.concatenate([jnp.where(m0, ch["bs"][:, sl], 0.0), jnp.where(m0, 0.0, ch["bs"][:, sl]),
                                  ch["ks"][:, sl], zc], axis=0)
            rb = jnp.concatenate([jnp.concatenate([a2, z2], axis=0),
                                  jnp.concatenate([ul, ch["vc"][:, sl].astype(BF16), zc.astype(BF16)], axis=0)],
                                 axis=1)
            md = _mm_tn(lb, rb)
            idx = ch["c"] * n_pair + j
            mb_s[idx] = md[:, :LANES].astype(BF16)
            dl_s[idx] = jnp.where(bd, md[:, LANES:], 0.0)
            wcol_s[idx] = jnp.broadcast_to(ch["wc"][:, sl], (LANES, LANES)).T
        return carry

    lax.fori_loop(0, tb // (RWKV_TRIP * C), phase_a, 0)

    def phase_b(c, carry):
        rs = pl.ds(pl.multiple_of(c * C, C), C)
        sts = [st_ref[j] for j in range(n_pair)]
        stb = [st.astype(BF16) for st in sts]
        upd = [_mm(mb_s[c * n_pair + j], stb[j]) for j in range(n_pair)]
        for j in range(n_pair):
            y_s[rs, sls[j]] += _mm(r2_s[rs, sls[j]], stb[j])
        for j in range(n_pair):
            idx = c * n_pair + j
            st_ref[j] = wcol_s[idx] * sts[j] + upd[j] + dl_s[idx]
        return carry

    lax.fori_loop(0, tb // C, phase_b, 0)

    y = y_s[...]
    inv = 1.0 / RWKV_HEAD
    mean = head_sum(y) * inv
    dlt = y - mean
    var = head_sum(dlt * dlt) * inv
    yn = dlt * lax.rsqrt(var + RWKV_GN_EPS) * lnw_ref[...] + lnb_ref[...]
    o_ref[...] = (yn + bon_s[...]) * g_s[...]


def _rwkv_mix(x, g, w, mu, w0, lora, a0, gup, kk, ka, rk, lnw, lnb):
    B, L, D = x.shape
    tb = min(L, 512)
    ones = jnp.kron(jnp.eye(LANES // RWKV_HEAD, dtype=F32), jnp.ones((RWKV_HEAD, RWKV_HEAD), F32)).astype(BF16)
    tri = jnp.tril(jnp.ones((CHUNK, CHUNK), F32)).astype(BF16)
    consts = (g, w, mu, w0, lora, a0, gup, kk, ka, rk, lnw, lnb, ones, tri)
    act = lambda: pltpu.VMEM((tb, MIX_W), F32)
    n_mat = (tb // CHUNK) * (RWKV_HEADS // 2)
    mat = lambda dt: pltpu.VMEM((n_mat, LANES, LANES), dt)
    return pl.pallas_call(
        _rwkv_kernel,
        grid=(B, L // tb),
        in_specs=[pl.BlockSpec((None, tb, D), lambda b, i: (b, i, 0))] + [_full(c.shape) for c in consts],
        out_specs=pl.BlockSpec((None, tb, MIX_W), lambda b, i: (b, i, 0)),
        out_shape=jax.ShapeDtypeStruct((B, L, MIX_W), F32),
        scratch_shapes=[pltpu.VMEM((8, RWKV_COLS), F32),
                        pltpu.VMEM((RWKV_HEADS // 2, LANES, LANES), F32)] + [act() for _ in range(10)]
        + [mat(BF16), mat(F32), mat(F32)],
        compiler_params=_params(("parallel", "arbitrary")),
        name="rwkv_mix",
    )(x, *consts)


def _gla_kernel(x_ref, g_ref, w_ref, gkup_ref, gkb_ref, ng_ref, tri_ref, o_ref,
                st_ref, q_s, k_s, v_s, gk_s, o_s):
    tb = x_ref.shape[0]
    C = CHUNK

    @pl.when(pl.program_id(1) == 0)
    def _():
        st_ref[...] = jnp.zeros_like(st_ref)

    p = _mm(_rms(x_ref[...], g_ref[...]), w_ref[...])
    q_s[...] = p[:, 0:GLA_QK] * (GLA_DK ** -0.5)
    k_s[...] = p[:, GLA_QK:2 * GLA_QK]
    v_s[...] = p[:, 2 * GLA_QK:2 * GLA_QK + MIX_W]
    go = p[:, 2 * GLA_QK + MIX_W:2 * GLA_QK + 2 * MIX_W]
    z = _mm(p[:, 2 * GLA_QK + 2 * MIX_W:], gkup_ref[...]) + gkb_ref[...]
    gk_s[...] = (jnp.minimum(z, 0.0) - jnp.log(1.0 + jnp.exp(-jnp.abs(z)))) * (1.0 / GLA_GATE_NORM)

    m0 = _iota((C, LANES), 1) < GLA_DK
    mv = _iota((C, 2 * GLA_DV), 1) < GLA_DV
    low = (_iota((2 * C, C), 0) % C) >= _iota((2 * C, C), 1)
    bd = (_iota((LANES, 2 * GLA_DV), 0) >= GLA_DK) == (_iota((LANES, 2 * GLA_DV), 1) >= GLA_DV)

    n_pair = GLA_HEADS // 2
    sls = [slice(LANES * j, LANES * (j + 1)) for j in range(n_pair)]

    def decayed(c):
        rs = pl.ds(pl.multiple_of(c * C, C), C)
        bcum = _mm_split_r(tri_ref[...], gk_s[rs, :])
        blast = bcum[C - 1:C, :]
        kc = k_s[rs, :]
        return dict(rs=rs, qd=q_s[rs, :] * jnp.exp(bcum), ki=kc * jnp.exp(-bcum),
                    ks=kc * jnp.exp(blast - bcum), el=jnp.exp(blast), vc=v_s[rs, :])

    def chunk_group(i, carry):
        chunks = [decayed(GLA_TRIP * i + q) for q in range(GLA_TRIP)]
        items = [(ch, j) for ch in chunks for j in range(n_pair)]
        vps = [ch["vc"][:, 2 * GLA_DV * j:2 * GLA_DV * (j + 1)].astype(BF16) for ch, j in items]
        atts = []
        for ch, j in items:
            qp = ch["qd"][:, sls[j]]
            l2 = jnp.concatenate([jnp.where(m0, qp, 0.0), jnp.where(m0, 0.0, qp)], axis=0)
            atts.append(jnp.where(low, _mm_nt(l2, ch["ki"][:, sls[j]]), 0.0))
        dss = [jnp.where(bd, _mm_tn(ch["ks"][:, sls[j]], vp), 0.0) for (ch, j), vp in zip(items, vps)]
        o2s = [_mm(att, vp) for att, vp in zip(atts, vps)]
        sts = [st_ref[j] for j in range(n_pair)]
        for n, (ch, j) in enumerate(items):
            sv = slice(2 * GLA_DV * j, 2 * GLA_DV * (j + 1))
            o_s[ch["rs"], sv] = jnp.where(mv, o2s[n][:C], o2s[n][C:]) + _mm(ch["qd"][:, sls[j]], sts[j])
            ecol = jnp.broadcast_to(ch["el"][:, sls[j]], (LANES, LANES)).T
            sts[j] = jnp.concatenate([ecol, ecol], axis=1) * sts[j] + dss[n]
        for j in range(n_pair):
            st_ref[j] = sts[j]
        return carry

    lax.fori_loop(0, tb // (GLA_TRIP * C), chunk_group, 0)

    o = o_s[...]
    outs = []
    for h in range(GLA_HEADS):
        oh = o[:, GLA_DV * h:GLA_DV * (h + 1)]
        outs.append(oh * lax.rsqrt(jnp.mean(oh * oh, axis=-1, keepdims=True) + 1e-5) * ng_ref[...])
    o_ref[...] = jnp.concatenate(outs, axis=1) * (go * _sigmoid(go))


def _gla_mix(x, g, w, gkup, gkb, ng):
    B, L, D = x.shape
    tb = min(L, 512)
    tri = jnp.tril(jnp.ones((CHUNK, CHUNK), F32)).astype(BF16)
    consts = (g, w, gkup, gkb, ng, tri)
    return pl.pallas_call(
        _gla_kernel,
        grid=(B, L // tb),
        in_specs=[pl.BlockSpec((None, tb, D), lambda b, i: (b, i, 0))] + [_full(c.shape) for c in consts],
        out_specs=pl.BlockSpec((None, tb, MIX_W), lambda b, i: (b, i, 0)),
        out_shape=jax.ShapeDtypeStruct((B, L, MIX_W), F32),
        scratch_shapes=[pltpu.VMEM((GLA_HEADS // 2, LANES, 2 * GLA_DV), F32),
                        pltpu.VMEM((tb, GLA_QK), F32), pltpu.VMEM((tb, GLA_QK), F32),
                        pltpu.VMEM((tb, MIX_W), F32), pltpu.VMEM((tb, GLA_QK), F32),
                        pltpu.VMEM((tb, MIX_W), F32)],
        compiler_params=_params(("parallel", "arbitrary")),
        name="gla_mix",
    )(x, *consts)


def _merge_kernel(x_ref, yrw_ref, ys5_ref, ygla_ref, g_ref, wg_ref, wb_ref, wo_ref, o_ref):
    x = x_ref[...]
    gates = _sigmoid(_mm(_rms(x, g_ref[...]), wg_ref[...]))
    merged = (gates[:, 0:D_MODEL] * _mm(yrw_ref[...], wb_ref[0])
              + gates[:, D_MODEL:2 * D_MODEL] * _mm(ys5_ref[...], wb_ref[1])
              + gates[:, 2 * D_MODEL:] * _mm(ygla_ref[...], wb_ref[2]))
    o_ref[...] = x + _mm(merged, wo_ref[...])


def _merge(x, y_rw, y_s5, y_gla, g, wg, wb, wo):
    B, L, D = x.shape
    tm = min(L, 512)
    bspec = lambda w: pl.BlockSpec((None, tm, w), lambda b, i: (b, i, 0))
    return pl.pallas_call(
        _merge_kernel,
        grid=(B, L // tm),
        in_specs=[bspec(D), bspec(MIX_W), bspec(MIX_W), bspec(MIX_W),
                  _full(g.shape), _full(wg.shape), _full(wb.shape), _full(wo.shape)],
        out_specs=bspec(D),
        out_shape=jax.ShapeDtypeStruct((B, L, D), F32),
        compiler_params=_params(("parallel", "parallel")),
        name="merge",
    )(x, y_rw, y_s5, y_gla, g, wg, wb, wo)


def _xattn_kernel(x_ref, kv_ref, g_ref, wq_ref, wo_ref, o_ref):
    x = x_ref[...]
    q = _mm(_rms(x, g_ref[...]), wq_ref[...])
    outs = []
    for h in range(XA_HEADS):
        kh = kv_ref[:, XA_HEAD * h:XA_HEAD * (h + 1)]
        vh = kv_ref[:, D_MODEL + XA_HEAD * h:D_MODEL + XA_HEAD * (h + 1)]
        s = _mm_nt(q[:, XA_HEAD * h:XA_HEAD * (h + 1)], kh) * (XA_HEAD ** -0.5)
        e = jnp.exp(s - jnp.max(s, axis=-1, keepdims=True))
        outs.append(_mm(e / jnp.sum(e, axis=-1, keepdims=True), vh))
    o_ref[...] = x + _mm(jnp.concatenate(outs, axis=1), wo_ref[...])


def _xattn(x, kv, g, wq, wo):
    B, L, D = x.shape
    M = kv.shape[1]
    tq = min(L, 512)
    xspec = pl.BlockSpec((None, tq, D), lambda b, i: (b, i, 0))
    return pl.pallas_call(
        _xattn_kernel,
        grid=(B, L // tq),
        in_specs=[xspec, pl.BlockSpec((None, M, 2 * D), lambda b, i: (b, 0, 0)),
                  _full(g.shape), _full(wq.shape), _full(wo.shape)],
        out_specs=xspec,
        out_shape=jax.ShapeDtypeStruct((B, L, D), F32),
        compiler_params=_params(("parallel", "parallel")),
        name="xattn",
    )(x, kv, g, wq, wo)


FFN_HALO = 16


def _ffn_kernel(final_norm, x_ref, xh_ref, g_ref, gf_ref, wug_ref, wuv_ref, cw_ref, cb_ref, wd_ref,
                o_ref, hn_s, ug_a, uv_a, ug_b, uv_b):
    nj = wug_ref.shape[0]
    tm = x_ref.shape[0]
    halo = _rms(xh_ref[...], g_ref[...])
    halo = jnp.where(pl.program_id(1) == 0, 0.0, halo)
    hn_s[0:FFN_HALO, :] = halo.astype(BF16)
    hn_s[FFN_HALO:, :] = _rms(x_ref[...], g_ref[...]).astype(BF16)
    o_ref[...] = x_ref[...]

    def up(jb, wr_g, wr_v):
        hn = hn_s[...]
        wr_g[...] = jnp.dot(hn, wug_ref[jb], preferred_element_type=F32)
        wr_v[...] = jnp.dot(hn, wuv_ref[jb], preferred_element_type=F32)

    def conv(rd, jb):
        cw = cw_ref[jb]
        tap = lambda d: rd[pl.ds(FFN_HALO - d, tm), :]
        return cw[2:3, :] * tap(0) + cw[1:2, :] * tap(1) + cw[0:1, :] * tap(2) + cb_ref[jb]

    def down(jb, rd_g, rd_v):
        gate = conv(rd_g, jb)
        act = gate * _sigmoid(gate) * conv(rd_v, nj + jb)
        o_ref[...] += _mm(act, wd_ref[jb])

    up(0, ug_a, uv_a)

    def two_blocks(t, carry):
        up(2 * t + 1, ug_b, uv_b)
        down(2 * t, ug_a, uv_a)
        up(2 * t + 2, ug_a, uv_a)
        down(2 * t + 1, ug_b, uv_b)
        return carry

    lax.fori_loop(0, (nj - 1) // 2, two_blocks, 0)
    down(nj - 1, ug_a, uv_a)
    if final_norm:
        o_ref[...] = _rms(o_ref[...], gf_ref[...])


def _ffn(x, g, w_up, conv_w, conv_b, w_down, g_final, final_norm):
    B, L, D = x.shape
    tm = min(L, 1024)
    bn = 256
    nj = D_FF // bn
    assert nj % 2 == 1 and nj * bn == D_FF
    hb = tm // FFN_HALO
    wu = w_up.reshape(D, 2 * nj, bn).transpose(1, 0, 2)
    cw = conv_w.reshape(3, 2 * nj, bn).transpose(1, 0, 2)
    cb = conv_b.reshape(2 * nj, 1, bn)
    wd = w_down.reshape(nj, bn, D)
    xspec = pl.BlockSpec((None, tm, D), lambda b, i: (b, i, 0))
    resident = lambda a: pl.BlockSpec(a.shape, lambda b, i: (0,) * a.ndim, pipeline_mode=pl.Buffered(1))
    ubuf = lambda: pltpu.VMEM((FFN_HALO + tm, bn), F32)
    return pl.pallas_call(
        functools.partial(_ffn_kernel, final_norm),
        grid=(B, L // tm),
        in_specs=[xspec,
                  pl.BlockSpec((None, FFN_HALO, D), lambda b, i: (b, jnp.maximum(i * hb - 1, 0), 0)),
                  _full(g.shape), _full(g_final.shape),
                  pl.BlockSpec((nj, D, bn), lambda b, i: (0, 0, 0), pipeline_mode=pl.Buffered(1)),
                  pl.BlockSpec((nj, D, bn), lambda b, i: (1, 0, 0), pipeline_mode=pl.Buffered(1)),
                  resident(cw), resident(cb), resident(wd)],
        out_specs=xspec,
        out_shape=jax.ShapeDtypeStruct((B, L, D), F32),
        scratch_shapes=[pltpu.VMEM((FFN_HALO + tm, D), BF16), ubuf(), ubuf(), ubuf(), ubuf()],
        compiler_params=_params(("parallel", "parallel")),
        name="conv_ffn",
    )(x, x, g, g_final, wu, wu, cw, cb, wd)


def _row(v):
    return v.reshape(1, -1)


def kernel(x, mem, norm_mix, w_in, rw_mu, rw_w0, rw_w_up, rw_a0, rw_a_up, rw_g_up, rw_k_k, rw_k_a, rw_r_k, rw_ln_w, rw_ln_b, s5_a_re, s5_a_im, s5_b_re, s5_b_im, s5_c_re, s5_c_im, s5_d, s5_log_step, s5_w_glu, s5_b_glu, gla_gk_up, gla_gk_b, gla_norm, w_branch, w_out, norm_mem, norm_xattn, xa_wq, xa_wkv, xa_wo, norm_ffn, ffn_w_up, ffn_conv, ffn_conv_b, ffn_w_down, norm_final):
    B, L, D = x.shape
    o_s5 = RWKV_COLS
    o_gla = o_s5 + S5_COLS
    o_gate = o_gla + GLA_COLS
    for l in range(DEPTH):
        wl = w_in[l]
        g_mix = _row(norm_mix[l])
        tabs = _s5_tables(s5_a_re[l], s5_a_im[l], s5_b_re[l], s5_b_im[l], s5_c_re[l], s5_c_im[l],
                          s5_log_step[l])
        y_s5 = _s5_mix(x, g_mix, wl[:, o_s5:o_gla].astype(BF16), *tabs, _row(s5_d[l]),
                       s5_w_glu[l].astype(BF16), _row(s5_b_glu[l]))
        zeros = jnp.zeros((RWKV_DECAY_RANK, MIX_W), w_in.dtype)
        lora = jnp.concatenate([jnp.concatenate([rw_w_up[l], zeros], axis=1),
                                jnp.concatenate([zeros, rw_a_up[l]], axis=1)], axis=0).astype(BF16)
        y_rw = _rwkv_mix(x, g_mix, wl[:, :o_s5].astype(BF16), _row(rw_mu[l]), _row(rw_w0[l]), lora,
                         _row(rw_a0[l]), rw_g_up[l].astype(BF16), _row(rw_k_k[l]), _row(rw_k_a[l]),
                         _row(rw_r_k[l]), _row(rw_ln_w[l]), _row(rw_ln_b[l]))
        w_gla = wl[:, o_gla:o_gate]
        w_gla = jnp.concatenate(
            [w_gla[:, :2 * GLA_QK + MIX_W], w_gla[:, 2 * GLA_QK + MIX_W + GLA_GATE_RANK:],
             w_gla[:, 2 * GLA_QK + MIX_W:2 * GLA_QK + MIX_W + GLA_GATE_RANK],
             jnp.zeros((D, GLA_COLS_PAD - GLA_COLS), w_in.dtype)], axis=1).astype(BF16)
        gkup = jnp.concatenate([gla_gk_up[l], jnp.zeros((LANES - GLA_GATE_RANK, GLA_QK), w_in.dtype)],
                               axis=0).astype(BF16)
        y_gla = _gla_mix(x, g_mix, w_gla, gkup, _row(gla_gk_b[l]), _row(gla_norm[l]))
        x = _merge(x, y_rw, y_s5, y_gla, g_mix, wl[:, o_gate:].astype(BF16),
                   w_branch[l].astype(BF16), w_out[l].astype(BF16))
        kv = _kv_proj(mem, _row(norm_mem), xa_wkv[l].astype(BF16))
        x = _xattn(x, kv, _row(norm_xattn[l]), xa_wq[l].astype(BF16), xa_wo[l].astype(BF16))
        x = _ffn(x, _row(norm_ffn[l]), ffn_w_up[l].astype(BF16), ffn_conv[l], _row(ffn_conv_b[l]),
                 ffn_w_down[l].astype(BF16), _row(norm_final), l == DEPTH - 1)
    return x
```

```python
import functools
import math

import jax
import jax.numpy as jnp
from jax import lax
from jax.experimental import pallas as pl
from jax.experimental.pallas import tpu as pltpu

F32 = jnp.float32
BF16 = jnp.bfloat16

D_MODEL = 1024
DEPTH = 4
N_BRANCH = 3
MIX_W = 512

RWKV_HEAD = 64
RWKV_HEADS = 8
RWKV_DECAY_RANK = 64
RWKV_A_RANK = 64
RWKV_G_RANK = 128
RWKV_GN_EPS = RWKV_HEAD * 1e-5
RWKV_COLS = 3 * MIX_W + RWKV_DECAY_RANK + RWKV_A_RANK + RWKV_G_RANK

S5_GROUP = 16
S5_GROUPS = 32
S5_STATE = 64
S5_COLS = MIX_W
S5_NS = S5_GROUPS * S5_STATE

GLA_HEADS = 4
GLA_DV = 128
GLA_DK = 64
GLA_GATE_RANK = 16
GLA_GATE_NORM = 16.0
GLA_QK = GLA_HEADS * GLA_DK
GLA_COLS = 2 * GLA_QK + 2 * MIX_W + GLA_GATE_RANK
GLA_COLS_PAD = 1664

XA_HEADS = 4
XA_HEAD = 256
D_FF = 2816
NORM_EPS = 1e-6

LANES = 128
CHUNK = 64
RWKV_TRIP = 4
GLA_TRIP = 4
EXP_M05 = math.exp(-0.5)
GELU_C = math.sqrt(2.0 / math.pi)

VMEM_LIMIT = 56 * 1024 * 1024


def _mm(a, b):
    return jnp.dot(a.astype(BF16), b.astype(BF16), preferred_element_type=F32)


def _mm_nt(a, b):
    return lax.dot_general(a.astype(BF16), b.astype(BF16), (((1,), (1,)), ((), ())),
                           preferred_element_type=F32)


def _mm_tn(a, b):
    return lax.dot_general(a.astype(BF16), b.astype(BF16), (((0,), (0,)), ((), ())),
                           preferred_element_type=F32)


def _split(a):
    hi = a.astype(BF16)
    lo = (a - hi.astype(F32)).astype(BF16)
    return hi, lo


def _mm_split_l(a, b_exact):
    hi, lo = _split(a)
    return (jnp.dot(hi, b_exact, preferred_element_type=F32)
            + jnp.dot(lo, b_exact, preferred_element_type=F32))


def _mm_split_r(a_exact, b):
    hi, lo = _split(b)
    return (jnp.dot(a_exact, hi, preferred_element_type=F32)
            + jnp.dot(a_exact, lo, preferred_element_type=F32))


def _sigmoid(x):
    return 1.0 / (1.0 + jnp.exp(-x))


def _rms(x, g):
    ms = jnp.mean(x * x, axis=-1, keepdims=True)
    return x * lax.rsqrt(ms + NORM_EPS) * g


def _iota(shape, dim):
    return lax.broadcasted_iota(jnp.int32, shape, dim)


def _params(sem):
    return pltpu.CompilerParams(dimension_semantics=sem, vmem_limit_bytes=VMEM_LIMIT)


def _full(shape):
    nd = len(shape)
    return pl.BlockSpec(shape, lambda *_: (0,) * nd)


def _norm_mm_kernel(x_ref, g_ref, w_ref, o_ref):
    o_ref[...] = _mm(_rms(x_ref[...], g_ref[...]), w_ref[...]).astype(o_ref.dtype)


def _kv_proj(mem, g, w):
    B, M, D = mem.shape
    N = w.shape[1]
    return pl.pallas_call(
        _norm_mm_kernel,
        grid=(B,),
        in_specs=[pl.BlockSpec((None, M, D), lambda b: (b, 0, 0)),
                  _full((1, D)), _full((D, N))],
        out_specs=pl.BlockSpec((None, M, N), lambda b: (b, 0, 0)),
        out_shape=jax.ShapeDtypeStruct((B, M, N), BF16),
        compiler_params=_params(("parallel",)),
        name="kv_proj",
    )(mem, g, w)


def _s5_kernel(x_ref, g_ref, w_ref, are_ref, aim_ref, bre_ref, bim_ref, cre_ref, cim_ref, d_ref,
               wglu_ref, bglu_ref, o_ref, sre, sim, st_re, st_im):
    nb, tt, D = x_ref.shape
    rows = nb * tt

    @pl.when(pl.program_id(0) == 0)
    def _():
        st_re[...] = jnp.zeros_like(st_re)
        st_im[...] = jnp.zeros_like(st_im)

    p = _mm(_rms(x_ref[...].reshape(rows, D), g_ref[...]), w_ref[...])
    u = pltpu.einshape("btc->tbc", p.reshape(nb, tt, S5_COLS)).reshape(rows, S5_COLS)
    half = S5_NS // 2
    for kt in range(2):
        uk = u[:, 256 * kt:256 * (kt + 1)].astype(BF16)
        sre[:, half * kt:half * (kt + 1)] = jnp.dot(uk, bre_ref[kt], preferred_element_type=F32)
        sim[:, half * kt:half * (kt + 1)] = jnp.dot(uk, bim_ref[kt], preferred_element_type=F32)

    cw = 512
    for cg in range(S5_NS // cw):
        cs = slice(cw * cg, cw * (cg + 1))
        ar = jnp.broadcast_to(are_ref[:, cs], (nb, cw))
        ai = jnp.broadcast_to(aim_ref[:, cs], (nb, cw))

        def body(t, carry):
            re, im = carry
            rs = pl.ds(pl.multiple_of(t * nb, nb), nb)
            nre = ar * re - ai * im + sre[rs, cs]
            nim = ar * im + ai * re + sim[rs, cs]
            sre[rs, cs] = nre
            sim[rs, cs] = nim
            return nre, nim

        re, im = lax.fori_loop(0, tt, body, (st_re[:, cs], st_im[:, cs]), unroll=4)
        st_re[:, cs] = re
        st_im[:, cs] = im

    ys = []
    for jt in range(2):
        hs = slice(half * jt, half * (jt + 1))
        ys.append(_mm(sre[:, hs], cre_ref[jt]) + _mm(sim[:, hs], cim_ref[jt]))
    y = jnp.concatenate(ys, axis=1) + d_ref[...] * u
    y = 0.5 * y * (1.0 + jnp.tanh(GELU_C * (y + 0.044715 * (y * y * y))))
    y = y * _sigmoid(_mm(y, wglu_ref[...]) + bglu_ref[...])
    o_ref[...] = pltpu.einshape("tbc->btc", y.reshape(tt, nb, S5_COLS))


def _s5_mix(x, g, w, are, aim, bre, bim, cre, cim, d, wglu, bglu):
    B, L, D = x.shape
    tt = min(L, 32)
    rows = tt * B
    consts = (g, w, are, aim, bre, bim, cre, cim, d, wglu, bglu)
    return pl.pallas_call(
        _s5_kernel,
        grid=(L // tt,),
        in_specs=[pl.BlockSpec((B, tt, D), lambda i: (0, i, 0))] + [_full(c.shape) for c in consts],
        out_specs=pl.BlockSpec((B, tt, S5_COLS), lambda i: (0, i, 0)),
        out_shape=jax.ShapeDtypeStruct((B, L, S5_COLS), F32),
        scratch_shapes=[pltpu.VMEM((rows, S5_NS), F32), pltpu.VMEM((rows, S5_NS), F32),
                        pltpu.VMEM((B, S5_NS), F32), pltpu.VMEM((B, S5_NS), F32)],
        compiler_params=_params(("arbitrary",)),
        name="s5_mix",
    )(x, *consts)


def _s5_tables(a_re, a_im, b_re, b_im, c_re, c_im, log_step):
    dt = jnp.exp(log_step)[:, None]
    mag = jnp.exp(dt * a_re)
    ang = dt * a_im
    abar_re, abar_im = mag * jnp.cos(ang), mag * jnp.sin(ang)
    den = a_re * a_re + a_im * a_im
    nr, ni = abar_re - 1.0, abar_im
    f_re = ((nr * a_re + ni * a_im) / den)[..., None]
    f_im = ((ni * a_re - nr * a_im) / den)[..., None]
    bbar_re = f_re * b_re - f_im * b_im
    bbar_im = f_re * b_im + f_im * b_re
    gl = S5_GROUPS // 2
    eye = jnp.eye(gl, dtype=a_re.dtype)

    def bmat(bb):
        bb = bb.reshape(2, gl, S5_STATE, S5_GROUP)
        return jnp.einsum('kgnc,gh->kgchn', bb, eye).reshape(2, gl * S5_GROUP, gl * S5_STATE)

    def cmat(cc):
        cc = cc.reshape(2, gl, S5_GROUP, S5_STATE)
        return jnp.einsum('kgcn,gh->kgnhc', cc, eye).reshape(2, gl * S5_STATE, gl * S5_GROUP)

    return (abar_re.reshape(1, S5_NS), abar_im.reshape(1, S5_NS),
            bmat(bbar_re).astype(BF16), bmat(bbar_im).astype(BF16),
            cmat(c_re).astype(BF16), cmat(-c_im).astype(BF16))


def _rwkv_kernel(x_ref, g_ref, w_ref, mu_ref, w0_ref, lora_ref, a0_ref, gup_ref, kk_ref, ka_ref,
                 rk_ref, lnw_ref, lnb_ref, ones_ref, tri_ref, o_ref,
                 prev_ref, st_ref, ld_s, al_s, be_s, k_s, r_s, v_s, y_s, bon_s, g_s, r2_s,
                 mb_s, dl_s, wcol_s):
    tb = x_ref.shape[0]
    C = CHUNK

    @pl.when(pl.program_id(1) == 0)
    def _():
        prev_ref[...] = jnp.zeros_like(prev_ref)
        st_ref[...] = jnp.zeros_like(st_ref)

    p = _mm(_rms(x_ref[...], g_ref[...]), w_ref[...])
    shifted = jnp.where(_iota(p.shape, 0) == 0, prev_ref[0:1, :], pltpu.roll(p, 1, axis=0))
    prev_ref[0:1, :] = p[tb - 1:tb, :]
    p = p + (shifted - p) * mu_ref[...]

    r = p[:, 0:MIX_W]
    k = p[:, MIX_W:2 * MIX_W]
    v = p[:, 2 * MIX_W:3 * MIX_W]
    lin = p[:, 3 * MIX_W:3 * MIX_W + LANES]
    gd = p[:, 3 * MIX_W + LANES:]
    lin = jnp.where(_iota(lin.shape, 1) < RWKV_DECAY_RANK, jnp.tanh(lin), lin)
    proj = _mm(lin, lora_ref[...])
    ld_s[...] = -EXP_M05 * _sigmoid(w0_ref[...] + proj[:, :MIX_W])
    a = _sigmoid(a0_ref[...] + proj[:, MIX_W:])
    g_s[...] = _mm(_sigmoid(gd), gup_ref[...])
    kkv = k * kk_ref[...]
    def head_sum(t):
        return jnp.concatenate([_mm(t[:, LANES * i:LANES * (i + 1)], ones_ref[...])
                                for i in range(MIX_W // LANES)], axis=1)

    kk = kkv * lax.rsqrt(head_sum(kkv * kkv) + 1e-12)
    k2 = k * (1.0 + (a - 1.0) * ka_ref[...])
    bon_s[...] = head_sum(r * k2 * rk_ref[...]) * v
    al_s[...] = -kk
    be_s[...] = kk * a
    k_s[...] = k2
    r_s[...] = r
    v_s[...] = v

    m0 = _iota((C, LANES), 1) < RWKV_HEAD
    ri = _iota((2 * C, 2 * C), 0)
    ci = _iota((2 * C, 2 * C), 1)
    same = (ri >= C) == (ci >= C)
    strict = same & (ri > ci)
    incl = same & (ri >= ci)
    top = ri < C
    eye = (ri == ci).astype(F32)
    bd = (_iota((LANES, LANES), 0) >= RWKV_HEAD) == (_iota((LANES, LANES), 1) >= RWKV_HEAD)
    zc = jnp.zeros((C, LANES), F32)
    z2 = jnp.zeros((2 * C, LANES), BF16)
    n_lvl = C.bit_length() - 1
    n_pair = RWKV_HEADS // 2
    sls = [slice(LANES * j, LANES * (j + 1)) for j in range(n_pair)]

    def decayed(c):
        rs = pl.ds(pl.multiple_of(c * C, C), C)
        ldc = ld_s[rs, :]
        cum = _mm_split_r(tri_ref[...], ldc)
        tot = cum[C - 1:C, :]
        e_neg = jnp.exp(-cum)
        e_st = jnp.exp(tot - cum)
        be = be_s[rs, :]
        kc = k_s[rs, :]
        return dict(rs=rs, c=c, al=al_s[rs, :] * jnp.exp(cum - ldc), bt=be * e_neg, bs=be * e_st,
                    kt=kc * e_neg, ks=kc * e_st, rt=r_s[rs, :] * jnp.exp(cum), vc=v_s[rs, :],
                    wc=jnp.exp(tot))

    def phase_a(i, carry):
        chunks = [decayed(RWKV_TRIP * i + q) for q in range(RWKV_TRIP)]
        items = [(ch, j) for ch in chunks for j in range(n_pair)]
        l4s, gms = [], []
        for ch, j in items:
            sl = sls[j]
            ap, rp = ch["al"][:, sl], ch["rt"][:, sl]
            l4 = jnp.concatenate([jnp.where(m0, ap, 0.0), jnp.where(m0, 0.0, ap),
                                  jnp.where(m0, rp, 0.0), jnp.where(m0, 0.0, rp)], axis=0).astype(BF16)
            r2 = jnp.concatenate([ch["bt"][:, sl], ch["kt"][:, sl]], axis=0)
            l4s.append(l4)
            gms.append(_mm_nt(l4, r2))
        gas = [gm[:2 * C] for gm in gms]
        grs = [gm[2 * C:] for gm in gms]
        gas_sw = [pltpu.roll(ga, C, axis=1) for ga in gas]
        grs_sw = [pltpu.roll(gr, C, axis=1) for gr in grs]
        v2s = [jnp.concatenate([ch["vc"][:, sls[j]], ch["vc"][:, sls[j]]], axis=0).astype(BF16)
               for ch, j in items]
        w1s = [_mm(jnp.where(strict, jnp.where(top, sw, ga), 0.0), v2) for ga, sw, v2 in zip(gas, gas_sw, v2s)]
        ns = [jnp.where(strict, jnp.where(top, ga, sw), 0.0) for ga, sw in zip(gas, gas_sw)]
        ts = [eye + n for n in ns]
        pws = [n.astype(BF16) for n in ns]
        for _ in range(1, n_lvl):
            pws = [_mm(pw, pw).astype(BF16) for pw in pws]
            ts = [t + _mm(pw, t) for pw, t in zip(pws, ts)]
        xs = [_mm(t, jnp.concatenate([w1.astype(BF16), l4[:2 * C]], axis=1))
              for t, w1, l4 in zip(ts, w1s, l4s)]
        uls = [x[:, :LANES].astype(BF16) for x in xs]
        a2s = [x[:, LANES:].astype(BF16) for x in xs]
        arbs = [jnp.where(incl, jnp.where(top, gr, sw), 0.0).astype(BF16) for gr, sw in zip(grs, grs_sw)]
        arks = [jnp.where(incl, jnp.where(top, sw, gr), 0.0).astype(BF16) for gr, sw in zip(grs, grs_sw)]
        yl2s = [_mm(jnp.concatenate([arb, ark], axis=1), jnp.concatenate([ul, v2], axis=0))
                for arb, ark, ul, v2 in zip(arbs, arks, uls, v2s)]
        ras = [_mm(arb, a2) for arb, a2 in zip(arbs, a2s)]
        for (ch, j), yl2, ra in zip(items, yl2s, ras):
            y_s[ch["rs"], sls[j]] = jnp.where(m0, yl2[:C], yl2[C:])
            r2_s[ch["rs"], sls[j]] = ra[:C] + ra[C:] + ch["rt"][:, sls[j]]
        for (ch, j), ul, a2 in zip(items, uls, a2s):
            sl = sls[j]
            lb = jnp.concatenate([jnp.where(m0, ch["bs"][:, sl], 0.0), jnp.where(m0, 0.0, ch["bs"][:, sl]),
                                  ch["ks"][:, sl], zc], axis=0)
            rb = jnp.concatenate([jnp.concatenate([a2, z2], axis=0),
                                  jnp.concatenate([ul, ch["vc"][:, sl].astype(BF16), zc.astype(BF16)], axis=0)],
                                 axis=1)
            md = _mm_tn(lb, rb)
            idx = ch["c"] * n_pair + j
            mb_s[idx] = md[:, :LANES].astype(BF16)
            dl_s[idx] = jnp.where(bd, md[:, LANES:], 0.0)
            wcol_s[idx] = jnp.broadcast_to(ch["wc"][:, sl], (LANES, LANES)).T
        return carry

    lax.fori_loop(0, tb // (RWKV_TRIP * C), phase_a, 0)

    def phase_b(c, carry):
        rs = pl.ds(pl.multiple_of(c * C, C), C)
        sts = [st_ref[j] for j in range(n_pair)]
        stb = [st.astype(BF16) for st in sts]
        upd = [_mm(mb_s[c * n_pair + j], stb[j]) for j in range(n_pair)]
        for j in range(n_pair):
            y_s[rs, sls[j]] += _mm(r2_s[rs, sls[j]], stb[j])
        for j in range(n_pair):
            idx = c * n_pair + j
            st_ref[j] = wcol_s[idx] * sts[j] + upd[j] + dl_s[idx]
        return carry

    lax.fori_loop(0, tb // C, phase_b, 0, unroll=True)

    y = y_s[...]
    inv = 1.0 / RWKV_HEAD
    mean = head_sum(y) * inv
    dlt = y - mean
    var = head_sum(dlt * dlt) * inv
    yn = dlt * lax.rsqrt(var + RWKV_GN_EPS) * lnw_ref[...] + lnb_ref[...]
    o_ref[...] = (yn + bon_s[...]) * g_s[...]


def _rwkv_mix(x, g, w, mu, w0, lora, a0, gup, kk, ka, rk, lnw, lnb):
    B, L, D = x.shape
    tb = min(L, 512)
    ones = jnp.kron(jnp.eye(LANES // RWKV_HEAD, dtype=F32), jnp.ones((RWKV_HEAD, RWKV_HEAD), F32)).astype(BF16)
    tri = jnp.tril(jnp.ones((CHUNK, CHUNK), F32)).astype(BF16)
    consts = (g, w, mu, w0, lora, a0, gup, kk, ka, rk, lnw, lnb, ones, tri)
    act = lambda: pltpu.VMEM((tb, MIX_W), F32)
    n_mat = (tb // CHUNK) * (RWKV_HEADS // 2)
    mat = lambda dt: pltpu.VMEM((n_mat, LANES, LANES), dt)
    return pl.pallas_call(
        _rwkv_kernel,
        grid=(B, L // tb),
        in_specs=[pl.BlockSpec((None, tb, D), lambda b, i: (b, i, 0))] + [_full(c.shape) for c in consts],
        out_specs=pl.BlockSpec((None, tb, MIX_W), lambda b, i: (b, i, 0)),
        out_shape=jax.ShapeDtypeStruct((B, L, MIX_W), F32),
        scratch_shapes=[pltpu.VMEM((8, RWKV_COLS), F32),
                        pltpu.VMEM((RWKV_HEADS // 2, LANES, LANES), F32)] + [act() for _ in range(10)]
        + [mat(BF16), mat(F32), mat(F32)],
        compiler_params=_params(("parallel", "arbitrary")),
        name="rwkv_mix",
    )(x, *consts)


def _gla_kernel(x_ref, g_ref, w_ref, gkup_ref, gkb_ref, ng_ref, tri_ref, o_ref,
                st_ref, q_s, k_s, v_s, gk_s, o_s):
    tb = x_ref.shape[0]
    C = CHUNK

    @pl.when(pl.program_id(1) == 0)
    def _():
        st_ref[...] = jnp.zeros_like(st_ref)

    p = _mm(_rms(x_ref[...], g_ref[...]), w_ref[...])
    q_s[...] = p[:, 0:GLA_QK] * (GLA_DK ** -0.5)
    k_s[...] = p[:, GLA_QK:2 * GLA_QK]
    v_s[...] = p[:, 2 * GLA_QK:2 * GLA_QK + MIX_W]
    go = p[:, 2 * GLA_QK + MIX_W:2 * GLA_QK + 2 * MIX_W]
    z = _mm(p[:, 2 * GLA_QK + 2 * MIX_W:], gkup_ref[...]) + gkb_ref[...]
    gk_s[...] = (jnp.minimum(z, 0.0) - jnp.log(1.0 + jnp.exp(-jnp.abs(z)))) * (1.0 / GLA_GATE_NORM)

    m0 = _iota((C, LANES), 1) < GLA_DK
    mv = _iota((C, 2 * GLA_DV), 1) < GLA_DV
    low = (_iota((2 * C, C), 0) % C) >= _iota((2 * C, C), 1)
    bd = (_iota((LANES, 2 * GLA_DV), 0) >= GLA_DK) == (_iota((LANES, 2 * GLA_DV), 1) >= GLA_DV)

    n_pair = GLA_HEADS // 2
    sls = [slice(LANES * j, LANES * (j + 1)) for j in range(n_pair)]

    def decayed(c):
        rs = pl.ds(pl.multiple_of(c * C, C), C)
        bcum = _mm_split_r(tri_ref[...], gk_s[rs, :])
        blast = bcum[C - 1:C, :]
        kc = k_s[rs, :]
        return dict(rs=rs, qd=q_s[rs, :] * jnp.exp(bcum), ki=kc * jnp.exp(-bcum),
                    ks=kc * jnp.exp(blast - bcum), el=jnp.exp(blast), vc=v_s[rs, :])

    def chunk_group(i, carry):
        chunks = [decayed(GLA_TRIP * i + q) for q in range(GLA_TRIP)]
        items = [(ch, j) for ch in chunks for j in range(n_pair)]
        vps = [ch["vc"][:, 2 * GLA_DV * j:2 * GLA_DV * (j + 1)].astype(BF16) for ch, j in items]
        atts = []
        for ch, j in items:
            qp = ch["qd"][:, sls[j]]
            l2 = jnp.concatenate([jnp.where(m0, qp, 0.0), jnp.where(m0, 0.0, qp)], axis=0)
            atts.append(jnp.where(low, _mm_nt(l2, ch["ki"][:, sls[j]]), 0.0))
        dss = [jnp.where(bd, _mm_tn(ch["ks"][:, sls[j]], vp), 0.0) for (ch, j), vp in zip(items, vps)]
        o2s = [_mm(att, vp) for att, vp in zip(atts, vps)]
        sts = [st_ref[j] for j in range(n_pair)]
        for n, (ch, j) in enumerate(items):
            sv = slice(2 * GLA_DV * j, 2 * GLA_DV * (j + 1))
            o_s[ch["rs"], sv] = jnp.where(mv, o2s[n][:C], o2s[n][C:]) + _mm(ch["qd"][:, sls[j]], sts[j])
            ecol = jnp.broadcast_to(ch["el"][:, sls[j]], (LANES, LANES)).T
            sts[j] = jnp.concatenate([ecol, ecol], axis=1) * sts[j] + dss[n]
        for j in range(n_pair):
            st_ref[j] = sts[j]
        return carry

    lax.fori_loop(0, tb // (GLA_TRIP * C), chunk_group, 0)

    o = o_s[...]
    outs = []
    for h in range(GLA_HEADS):
        oh = o[:, GLA_DV * h:GLA_DV * (h + 1)]
        outs.append(oh * lax.rsqrt(jnp.mean(oh * oh, axis=-1, keepdims=True) + 1e-5) * ng_ref[...])
    o_ref[...] = jnp.concatenate(outs, axis=1) * (go * _sigmoid(go))


def _gla_mix(x, g, w, gkup, gkb, ng):
    B, L, D = x.shape
    tb = min(L, 512)
    tri = jnp.tril(jnp.ones((CHUNK, CHUNK), F32)).astype(BF16)
    consts = (g, w, gkup, gkb, ng, tri)
    return pl.pallas_call(
        _gla_kernel,
        grid=(B, L // tb),
        in_specs=[pl.BlockSpec((None, tb, D), lambda b, i: (b, i, 0))] + [_full(c.shape) for c in consts],
        out_specs=pl.BlockSpec((None, tb, MIX_W), lambda b, i: (b, i, 0)),
        out_shape=jax.ShapeDtypeStruct((B, L, MIX_W), F32),
        scratch_shapes=[pltpu.VMEM((GLA_HEADS // 2, LANES, 2 * GLA_DV), F32),
                        pltpu.VMEM((tb, GLA_QK), F32), pltpu.VMEM((tb, GLA_QK), F32),
                        pltpu.VMEM((tb, MIX_W), F32), pltpu.VMEM((tb, GLA_QK), F32),
                        pltpu.VMEM((tb, MIX_W), F32)],
        compiler_params=_params(("parallel", "arbitrary")),
        name="gla_mix",
    )(x, *consts)


def _merge_kernel(x_ref, yrw_ref, ys5_ref, ygla_ref, g_ref, wg_ref, wb_ref, wo_ref, o_ref):
    x = x_ref[...]
    gates = _sigmoid(_mm(_rms(x, g_ref[...]), wg_ref[...]))
    merged = (gates[:, 0:D_MODEL] * _mm(yrw_ref[...], wb_ref[0])
              + gates[:, D_MODEL:2 * D_MODEL] * _mm(ys5_ref[...], wb_ref[1])
              + gates[:, 2 * D_MODEL:] * _mm(ygla_ref[...], wb_ref[2]))
    o_ref[...] = x + _mm(merged, wo_ref[...])


def _merge(x, y_rw, y_s5, y_gla, g, wg, wb, wo):
    B, L, D = x.shape
    tm = min(L, 512)
    bspec = lambda w: pl.BlockSpec((None, tm, w), lambda b, i: (b, i, 0))
    return pl.pallas_call(
        _merge_kernel,
        grid=(B, L // tm),
        in_specs=[bspec(D), bspec(MIX_W), bspec(MIX_W), bspec(MIX_W),
                  _full(g.shape), _full(wg.shape), _full(wb.shape), _full(wo.shape)],
        out_specs=bspec(D),
        out_shape=jax.ShapeDtypeStruct((B, L, D), F32),
        compiler_params=_params(("parallel", "parallel")),
        name="merge",
    )(x, y_rw, y_s5, y_gla, g, wg, wb, wo)


def _xattn_kernel(x_ref, kv_ref, g_ref, wq_ref, wo_ref, o_ref):
    x = x_ref[...]
    q = _mm(_rms(x, g_ref[...]), wq_ref[...])
    hs = [slice(XA_HEAD * h, XA_HEAD * (h + 1)) for h in range(XA_HEADS)]
    ss = [_mm_nt(q[:, sl], kv_ref[:, sl]) * (XA_HEAD ** -0.5) for sl in hs]
    es = [jnp.exp(s - jnp.max(s, axis=-1, keepdims=True)) for s in ss]
    pvs = [_mm(e, kv_ref[:, D_MODEL + XA_HEAD * h:D_MODEL + XA_HEAD * (h + 1)]) for h, e in enumerate(es)]
    outs = [pv / jnp.sum(e, axis=-1, keepdims=True) for pv, e in zip(pvs, es)]
    o_ref[...] = x + _mm(jnp.concatenate(outs, axis=1), wo_ref[...])


def _xattn(x, kv, g, wq, wo):
    B, L, D = x.shape
    M = kv.shape[1]
    tq = min(L, 512)
    xspec = pl.BlockSpec((None, tq, D), lambda b, i: (b, i, 0))
    return pl.pallas_call(
        _xattn_kernel,
        grid=(B, L // tq),
        in_specs=[xspec, pl.BlockSpec((None, M, 2 * D), lambda b, i: (b, 0, 0)),
                  _full(g.shape), _full(wq.shape), _full(wo.shape)],
        out_specs=xspec,
        out_shape=jax.ShapeDtypeStruct((B, L, D), F32),
        compiler_params=_params(("parallel", "parallel")),
        name="xattn",
    )(x, kv, g, wq, wo)


FFN_HALO = 16


def _ffn_kernel(final_norm, x_ref, xh_ref, g_ref, gf_ref, wug_ref, wuv_ref, cw_ref, cb_ref, wd_ref,
                o_ref, hn_s, ug_a, uv_a, ug_b, uv_b):
    nj = wug_ref.shape[0]
    tm = x_ref.shape[0]
    halo = _rms(xh_ref[...], g_ref[...])
    halo = jnp.where(pl.program_id(1) == 0, 0.0, halo)
    hn_s[0:FFN_HALO, :] = halo.astype(BF16)
    hn_s[FFN_HALO:, :] = _rms(x_ref[...], g_ref[...]).astype(BF16)
    o_ref[...] = x_ref[...]

    def up(jb, wr_g, wr_v):
        hn = hn_s[...]
        wr_g[...] = jnp.dot(hn, wug_ref[jb], preferred_element_type=F32)
        wr_v[...] = jnp.dot(hn, wuv_ref[jb], preferred_element_type=F32)

    def conv(rd, jb):
        cw = cw_ref[jb]
        tap = lambda d: rd[pl.ds(FFN_HALO - d, tm), :]
        return cw[2:3, :] * tap(0) + cw[1:2, :] * tap(1) + cw[0:1, :] * tap(2) + cb_ref[jb]

    def down(jb, rd_g, rd_v):
        gate = conv(rd_g, jb)
        act = gate * _sigmoid(gate) * conv(rd_v, nj + jb)
        o_ref[...] += _mm(act, wd_ref[jb])

    up(0, ug_a, uv_a)

    def two_blocks(t, carry):
        up(2 * t + 1, ug_b, uv_b)
        down(2 * t, ug_a, uv_a)
        up(2 * t + 2, ug_a, uv_a)
        down(2 * t + 1, ug_b, uv_b)
        return carry

    lax.fori_loop(0, (nj - 1) // 2, two_blocks, 0)
    down(nj - 1, ug_a, uv_a)
    if final_norm:
        o_ref[...] = _rms(o_ref[...], gf_ref[...])


def _ffn(x, g, w_up, conv_w, conv_b, w_down, g_final, final_norm):
    B, L, D = x.shape
    tm = min(L, 1024)
    bn = 256
    nj = D_FF // bn
    assert nj % 2 == 1 and nj * bn == D_FF
    hb = tm // FFN_HALO
    wu = w_up.reshape(D, 2 * nj, bn).transpose(1, 0, 2)
    cw = conv_w.reshape(3, 2 * nj, bn).transpose(1, 0, 2)
    cb = conv_b.reshape(2 * nj, 1, bn)
    wd = w_down.reshape(nj, bn, D)
    xspec = pl.BlockSpec((None, tm, D), lambda b, i: (b, i, 0))
    resident = lambda a: pl.BlockSpec(a.shape, lambda b, i: (0,) * a.ndim, pipeline_mode=pl.Buffered(1))
    ubuf = lambda: pltpu.VMEM((FFN_HALO + tm, bn), F32)
    return pl.pallas_call(
        functools.partial(_ffn_kernel, final_norm),
        grid=(B, L // tm),
        in_specs=[xspec,
                  pl.BlockSpec((None, FFN_HALO, D), lambda b, i: (b, jnp.maximum(i * hb - 1, 0), 0)),
                  _full(g.shape), _full(g_final.shape),
                  pl.BlockSpec((nj, D, bn), lambda b, i: (0, 0, 0), pipeline_mode=pl.Buffered(1)),
                  pl.BlockSpec((nj, D, bn), lambda b, i: (1, 0, 0), pipeline_mode=pl.Buffered(1)),
                  resident(cw), resident(cb), resident(wd)],
        out_specs=xspec,
        out_shape=jax.ShapeDtypeStruct((B, L, D), F32),
        scratch_shapes=[pltpu.VMEM((FFN_HALO + tm, D), BF16), ubuf(), ubuf(), ubuf(), ubuf()],
        compiler_params=_params(("parallel", "parallel")),
        name="conv_ffn",
    )(x, x, g, g_final, wu, wu, cw, cb, wd)


def _row(v):
    return v.reshape(1, -1)


def kernel(x, mem, norm_mix, w_in, rw_mu, rw_w0, rw_w_up, rw_a0, rw_a_up, rw_g_up, rw_k_k, rw_k_a, rw_r_k, rw_ln_w, rw_ln_b, s5_a_re, s5_a_im, s5_b_re, s5_b_im, s5_c_re, s5_c_im, s5_d, s5_log_step, s5_w_glu, s5_b_glu, gla_gk_up, gla_gk_b, gla_norm, w_branch, w_out, norm_mem, norm_xattn, xa_wq, xa_wkv, xa_wo, norm_ffn, ffn_w_up, ffn_conv, ffn_conv_b, ffn_w_down, norm_final):
    B, L, D = x.shape
    o_s5 = RWKV_COLS
    o_gla = o_s5 + S5_COLS
    o_gate = o_gla + GLA_COLS
    for l in range(DEPTH):
        wl = w_in[l]
        g_mix = _row(norm_mix[l])
        tabs = _s5_tables(s5_a_re[l], s5_a_im[l], s5_b_re[l], s5_b_im[l], s5_c_re[l], s5_c_im[l],
                          s5_log_step[l])
        y_s5 = _s5_mix(x, g_mix, wl[:, o_s5:o_gla].astype(BF16), *tabs, _row(s5_d[l]),
                       s5_w_glu[l].astype(BF16), _row(s5_b_glu[l]))
        zeros = jnp.zeros((RWKV_DECAY_RANK, MIX_W), w_in.dtype)
        lora = jnp.concatenate([jnp.concatenate([rw_w_up[l], zeros], axis=1),
                                jnp.concatenate([zeros, rw_a_up[l]], axis=1)], axis=0).astype(BF16)
        y_rw = _rwkv_mix(x, g_mix, wl[:, :o_s5].astype(BF16), _row(rw_mu[l]), _row(rw_w0[l]), lora,
                         _row(rw_a0[l]), rw_g_up[l].astype(BF16), _row(rw_k_k[l]), _row(rw_k_a[l]),
                         _row(rw_r_k[l]), _row(rw_ln_w[l]), _row(rw_ln_b[l]))
        w_gla = wl[:, o_gla:o_gate]
        w_gla = jnp.concatenate(
            [w_gla[:, :2 * GLA_QK + MIX_W], w_gla[:, 2 * GLA_QK + MIX_W + GLA_GATE_RANK:],
             w_gla[:, 2 * GLA_QK + MIX_W:2 * GLA_QK + MIX_W + GLA_GATE_RANK],
             jnp.zeros((D, GLA_COLS_PAD - GLA_COLS), w_in.dtype)], axis=1).astype(BF16)
        gkup = jnp.concatenate([gla_gk_up[l], jnp.zeros((LANES - GLA_GATE_RANK, GLA_QK), w_in.dtype)],
                               axis=0).astype(BF16)
        y_gla = _gla_mix(x, g_mix, w_gla, gkup, _row(gla_gk_b[l]), _row(gla_norm[l]))
        x = _merge(x, y_rw, y_s5, y_gla, g_mix, wl[:, o_gate:].astype(BF16),
                   w_branch[l].astype(BF16), w_out[l].astype(BF16))
        kv = _kv_proj(mem, _row(norm_mem), xa_wkv[l].astype(BF16))
        x = _xattn(x, kv, _row(norm_xattn[l]), xa_wq[l].astype(BF16), xa_wo[l].astype(BF16))
        x = _ffn(x, _row(norm_ffn[l]), ffn_w_up[l].astype(BF16), ffn_conv[l], _row(ffn_conv_b[l]),
                 ffn_w_down[l].astype(BF16), _row(norm_final), l == DEPTH - 1)
    return x
```

```python
import functools
import math

import jax
import jax.numpy as jnp
from jax import lax
from jax.experimental import pallas as pl
from jax.experimental.pallas import tpu as pltpu

F32 = jnp.float32
BF16 = jnp.bfloat16

D_MODEL = 1024
DEPTH = 4
N_BRANCH = 3
MIX_W = 512

RWKV_HEAD = 64
RWKV_HEADS = 8
RWKV_DECAY_RANK = 64
RWKV_A_RANK = 64
RWKV_G_RANK = 128
RWKV_GN_EPS = RWKV_HEAD * 1e-5
RWKV_COLS = 3 * MIX_W + RWKV_DECAY_RANK + RWKV_A_RANK + RWKV_G_RANK

S5_GROUP = 16
S5_GROUPS = 32
S5_STATE = 64
S5_COLS = MIX_W
S5_NS = S5_GROUPS * S5_STATE

GLA_HEADS = 4
GLA_DV = 128
GLA_DK = 64
GLA_GATE_RANK = 16
GLA_GATE_NORM = 16.0
GLA_QK = GLA_HEADS * GLA_DK
GLA_COLS = 2 * GLA_QK + 2 * MIX_W + GLA_GATE_RANK
GLA_COLS_PAD = 1664

XA_HEADS = 4
XA_HEAD = 256
D_FF = 2816
NORM_EPS = 1e-6

LANES = 128
CHUNK = 64
RWKV_TRIP = 4
GLA_TRIP = 4
EXP_M05 = math.exp(-0.5)
GELU_C = math.sqrt(2.0 / math.pi)

VMEM_LIMIT = 56 * 1024 * 1024


def _mm(a, b):
    return jnp.dot(a.astype(BF16), b.astype(BF16), preferred_element_type=F32)


def _mm_nt(a, b):
    return lax.dot_general(a.astype(BF16), b.astype(BF16), (((1,), (1,)), ((), ())),
                           preferred_element_type=F32)


def _mm_tn(a, b):
    return lax.dot_general(a.astype(BF16), b.astype(BF16), (((0,), (0,)), ((), ())),
                           preferred_element_type=F32)


def _split(a):
    hi = a.astype(BF16)
    lo = (a - hi.astype(F32)).astype(BF16)
    return hi, lo


def _mm_split_l(a, b_exact):
    hi, lo = _split(a)
    return (jnp.dot(hi, b_exact, preferred_element_type=F32)
            + jnp.dot(lo, b_exact, preferred_element_type=F32))


def _mm_split_r(a_exact, b):
    hi, lo = _split(b)
    return (jnp.dot(a_exact, hi, preferred_element_type=F32)
            + jnp.dot(a_exact, lo, preferred_element_type=F32))


def _sigmoid(x):
    return 1.0 / (1.0 + jnp.exp(-x))


def _rms(x, g):
    ms = jnp.mean(x * x, axis=-1, keepdims=True)
    return x * lax.rsqrt(ms + NORM_EPS) * g


def _iota(shape, dim):
    return lax.broadcasted_iota(jnp.int32, shape, dim)


def _params(sem):
    return pltpu.CompilerParams(dimension_semantics=sem, vmem_limit_bytes=VMEM_LIMIT)


def _full(shape):
    nd = len(shape)
    return pl.BlockSpec(shape, lambda *_: (0,) * nd)


def _resident(shape):
    nd = len(shape)
    return pl.BlockSpec(shape, lambda *_: (0,) * nd, pipeline_mode=pl.Buffered(1))


def _norm_mm_kernel(x_ref, g_ref, w_ref, o_ref):
    o_ref[...] = _mm(_rms(x_ref[...], g_ref[...]), w_ref[...]).astype(o_ref.dtype)


def _kv_proj(mem, g, w):
    B, M, D = mem.shape
    N = w.shape[1]
    return pl.pallas_call(
        _norm_mm_kernel,
        grid=(B,),
        in_specs=[pl.BlockSpec((None, M, D), lambda b: (b, 0, 0)),
                  _full((1, D)), _full((D, N))],
        out_specs=pl.BlockSpec((None, M, N), lambda b: (b, 0, 0)),
        out_shape=jax.ShapeDtypeStruct((B, M, N), BF16),
        compiler_params=_params(("parallel",)),
        name="kv_proj",
    )(mem, g, w)


def _s5_kernel(x_ref, g_ref, w_ref, are_ref, aim_ref, bre_ref, bim_ref, cre_ref, cim_ref, d_ref,
               wglu_ref, bglu_ref, o_ref, sre, sim, st_re, st_im):
    nb, tt, D = x_ref.shape
    rows = nb * tt

    @pl.when(pl.program_id(0) == 0)
    def _():
        st_re[...] = jnp.zeros_like(st_re)
        st_im[...] = jnp.zeros_like(st_im)

    p = _mm(_rms(x_ref[...].reshape(rows, D), g_ref[...]), w_ref[...])
    u = pltpu.einshape("btc->tbc", p.reshape(nb, tt, S5_COLS)).reshape(rows, S5_COLS)
    half = S5_NS // 2
    for kt in range(2):
        uk = u[:, 256 * kt:256 * (kt + 1)].astype(BF16)
        sre[:, half * kt:half * (kt + 1)] = jnp.dot(uk, bre_ref[kt], preferred_element_type=F32)
        sim[:, half * kt:half * (kt + 1)] = jnp.dot(uk, bim_ref[kt], preferred_element_type=F32)

    cw = 512
    for cg in range(S5_NS // cw):
        cs = slice(cw * cg, cw * (cg + 1))
        ar = jnp.broadcast_to(are_ref[:, cs], (nb, cw))
        ai = jnp.broadcast_to(aim_ref[:, cs], (nb, cw))

        def body(t, carry):
            re, im = carry
            rs = pl.ds(pl.multiple_of(t * nb, nb), nb)
            nre = ar * re - ai * im + sre[rs, cs]
            nim = ar * im + ai * re + sim[rs, cs]
            sre[rs, cs] = nre
            sim[rs, cs] = nim
            return nre, nim

        re, im = lax.fori_loop(0, tt, body, (st_re[:, cs], st_im[:, cs]), unroll=4)
        st_re[:, cs] = re
        st_im[:, cs] = im

    ys = []
    for jt in range(2):
        hs = slice(half * jt, half * (jt + 1))
        ys.append(_mm(sre[:, hs], cre_ref[jt]) + _mm(sim[:, hs], cim_ref[jt]))
    y = jnp.concatenate(ys, axis=1) + d_ref[...] * u
    y = 0.5 * y * (1.0 + jnp.tanh(GELU_C * (y + 0.044715 * (y * y * y))))
    y = y * _sigmoid(_mm(y, wglu_ref[...]) + bglu_ref[...])
    o_ref[...] = pltpu.einshape("tbc->btc", y.reshape(tt, nb, S5_COLS))


def _s5_mix(x, g, w, are, aim, bre, bim, cre, cim, d, wglu, bglu):
    B, L, D = x.shape
    tt = min(L, 64)
    rows = tt * B
    consts = (g, w, are, aim, bre, bim, cre, cim, d, wglu, bglu)
    return pl.pallas_call(
        _s5_kernel,
        grid=(L // tt,),
        in_specs=[pl.BlockSpec((B, tt, D), lambda i: (0, i, 0))] + [_full(c.shape) for c in consts],
        out_specs=pl.BlockSpec((B, tt, S5_COLS), lambda i: (0, i, 0)),
        out_shape=jax.ShapeDtypeStruct((B, L, S5_COLS), F32),
        scratch_shapes=[pltpu.VMEM((rows, S5_NS), F32), pltpu.VMEM((rows, S5_NS), F32),
                        pltpu.VMEM((B, S5_NS), F32), pltpu.VMEM((B, S5_NS), F32)],
        compiler_params=_params(("arbitrary",)),
        name="s5_mix",
    )(x, *consts)


def _s5_tables(a_re, a_im, b_re, b_im, c_re, c_im, log_step):
    dt = jnp.exp(log_step)[:, None]
    mag = jnp.exp(dt * a_re)
    ang = dt * a_im
    abar_re, abar_im = mag * jnp.cos(ang), mag * jnp.sin(ang)
    den = a_re * a_re + a_im * a_im
    nr, ni = abar_re - 1.0, abar_im
    f_re = ((nr * a_re + ni * a_im) / den)[..., None]
    f_im = ((ni * a_re - nr * a_im) / den)[..., None]
    bbar_re = f_re * b_re - f_im * b_im
    bbar_im = f_re * b_im + f_im * b_re
    gl = S5_GROUPS // 2
    eye = jnp.eye(gl, dtype=a_re.dtype)

    def bmat(bb):
        bb = bb.reshape(2, gl, S5_STATE, S5_GROUP)
        return jnp.einsum('kgnc,gh->kgchn', bb, eye).reshape(2, gl * S5_GROUP, gl * S5_STATE)

    def cmat(cc):
        cc = cc.reshape(2, gl, S5_GROUP, S5_STATE)
        return jnp.einsum('kgcn,gh->kgnhc', cc, eye).reshape(2, gl * S5_STATE, gl * S5_GROUP)

    return (abar_re.reshape(1, S5_NS), abar_im.reshape(1, S5_NS),
            bmat(bbar_re).astype(BF16), bmat(bbar_im).astype(BF16),
            cmat(c_re).astype(BF16), cmat(-c_im).astype(BF16))


def _rwkv_kernel(x_ref, g_ref, w_ref, mu_ref, w0_ref, lora_ref, a0_ref, gup_ref, kk_ref, ka_ref,
                 rk_ref, lnw_ref, lnb_ref, ones_ref, tri_ref, o_ref,
                 prev_ref, st_ref, ld_s, al_s, be_s, k_s, r_s, v_s, y_s, bon_s, g_s, r2_s,
                 mb_s, dl_s, wcol_s):
    tb = x_ref.shape[0]
    C = CHUNK

    @pl.when(pl.program_id(1) == 0)
    def _():
        prev_ref[...] = jnp.zeros_like(prev_ref)
        st_ref[...] = jnp.zeros_like(st_ref)

    p = _mm(_rms(x_ref[...], g_ref[...]), w_ref[...])
    shifted = jnp.where(_iota(p.shape, 0) == 0, prev_ref[0:1, :], pltpu.roll(p, 1, axis=0))
    prev_ref[0:1, :] = p[tb - 1:tb, :]
    p = p + (shifted - p) * mu_ref[...]

    r = p[:, 0:MIX_W]
    k = p[:, MIX_W:2 * MIX_W]
    v = p[:, 2 * MIX_W:3 * MIX_W]
    lin = p[:, 3 * MIX_W:3 * MIX_W + LANES]
    gd = p[:, 3 * MIX_W + LANES:]
    lin = jnp.where(_iota(lin.shape, 1) < RWKV_DECAY_RANK, jnp.tanh(lin), lin)
    proj = _mm(lin, lora_ref[...])
    ld_s[...] = -EXP_M05 * _sigmoid(w0_ref[...] + proj[:, :MIX_W])
    a = _sigmoid(a0_ref[...] + proj[:, MIX_W:])
    g_s[...] = _mm(_sigmoid(gd), gup_ref[...])
    kkv = k * kk_ref[...]
    def head_sum(t):
        return jnp.concatenate([_mm(t[:, LANES * i:LANES * (i + 1)], ones_ref[...])
                                for i in range(MIX_W // LANES)], axis=1)

    kk = kkv * lax.rsqrt(head_sum(kkv * kkv) + 1e-12)
    k2 = k * (1.0 + (a - 1.0) * ka_ref[...])
    bon_s[...] = head_sum(r * k2 * rk_ref[...]) * v
    al_s[...] = -kk
    be_s[...] = kk * a
    k_s[...] = k2
    r_s[...] = r
    v_s[...] = v

    m0 = _iota((C, LANES), 1) < RWKV_HEAD
    ri = _iota((2 * C, 2 * C), 0)
    ci = _iota((2 * C, 2 * C), 1)
    same = (ri >= C) == (ci >= C)
    strict = same & (ri > ci)
    incl = same & (ri >= ci)
    top = ri < C
    eye = (ri == ci).astype(F32)
    bd = (_iota((LANES, LANES), 0) >= RWKV_HEAD) == (_iota((LANES, LANES), 1) >= RWKV_HEAD)
    zc = jnp.zeros((C, LANES), F32)
    z2 = jnp.zeros((2 * C, LANES), BF16)
    n_lvl = C.bit_length() - 1
    n_pair = RWKV_HEADS // 2
    sls = [slice(LANES * j, LANES * (j + 1)) for j in range(n_pair)]

    def decayed(c):
        rs = pl.ds(pl.multiple_of(c * C, C), C)
        ldc = ld_s[rs, :]
        cum = _mm_split_r(tri_ref[...], ldc)
        tot = cum[C - 1:C, :]
        e_neg = jnp.exp(-cum)
        e_st = jnp.exp(tot - cum)
        be = be_s[rs, :]
        kc = k_s[rs, :]
        return dict(rs=rs, c=c, al=al_s[rs, :] * jnp.exp(cum - ldc), bt=be * e_neg, bs=be * e_st,
                    kt=kc * e_neg, ks=kc * e_st, rt=r_s[rs, :] * jnp.exp(cum), vc=v_s[rs, :],
                    wc=jnp.exp(tot))

    def phase_a(i, carry):
        chunks = [decayed(RWKV_TRIP * i + q) for q in range(RWKV_TRIP)]
        items = [(ch, j) for ch in chunks for j in range(n_pair)]
        l4s, gms = [], []
        for ch, j in items:
            sl = sls[j]
            ap, rp = ch["al"][:, sl], ch["rt"][:, sl]
            l4 = jnp.concatenate([jnp.where(m0, ap, 0.0), jnp.where(m0, 0.0, ap),
                                  jnp.where(m0, rp, 0.0), jnp.where(m0, 0.0, rp)], axis=0).astype(BF16)
            r2 = jnp.concatenate([ch["bt"][:, sl], ch["kt"][:, sl]], axis=0)
            l4s.append(l4)
            gms.append(_mm_nt(l4, r2))
        gas = [gm[:2 * C] for gm in gms]
        grs = [gm[2 * C:] for gm in gms]
        gas_sw = [pltpu.roll(ga, C, axis=1) for ga in gas]
        grs_sw = [pltpu.roll(gr, C, axis=1) for gr in grs]
        v2s = [jnp.concatenate([ch["vc"][:, sls[j]], ch["vc"][:, sls[j]]], axis=0).astype(BF16)
               for ch, j in items]
        w1s = [_mm(jnp.where(strict, jnp.where(top, sw, ga), 0.0), v2) for ga, sw, v2 in zip(gas, gas_sw, v2s)]
        ns = [jnp.where(strict, jnp.where(top, ga, sw), 0.0) for ga, sw in zip(gas, gas_sw)]
        ts = [eye + n for n in ns]
        pws = [n.astype(BF16) for n in ns]
        for _ in range(1, n_lvl):
            pws = [_mm(pw, pw).astype(BF16) for pw in pws]
            ts = [t + _mm(pw, t) for pw, t in zip(pws, ts)]
        xs = [_mm(t, jnp.concatenate([w1.astype(BF16), l4[:2 * C]], axis=1))
              for t, w1, l4 in zip(ts, w1s, l4s)]
        uls = [x[:, :LANES].astype(BF16) for x in xs]
        a2s = [x[:, LANES:].astype(BF16) for x in xs]
        arbs = [jnp.where(incl, jnp.where(top, gr, sw), 0.0).astype(BF16) for gr, sw in zip(grs, grs_sw)]
        arks = [jnp.where(incl, jnp.where(top, sw, gr), 0.0).astype(BF16) for gr, sw in zip(grs, grs_sw)]
        yl2s = [_mm(jnp.concatenate([arb, ark], axis=1), jnp.concatenate([ul, v2], axis=0))
                for arb, ark, ul, v2 in zip(arbs, arks, uls, v2s)]
        ras = [_mm(arb, a2) for arb, a2 in zip(arbs, a2s)]
        for (ch, j), yl2, ra in zip(items, yl2s, ras):
            y_s[ch["rs"], sls[j]] = jnp.where(m0, yl2[:C], yl2[C:])
            r2_s[ch["rs"], sls[j]] = ra[:C] + ra[C:] + ch["rt"][:, sls[j]]
        for (ch, j), ul, a2 in zip(items, uls, a2s):
            sl = sls[j]
            lb = jnp.concatenate([jnp.where(m0, ch["bs"][:, sl], 0.0), jnp.where(m0, 0.0, ch["bs"][:, sl]),
                                  ch["ks"][:, sl], zc], axis=0)
            rb = jnp.concatenate([jnp.concatenate([a2, z2], axis=0),
                                  jnp.concatenate([ul, ch["vc"][:, sl].astype(BF16), zc.astype(BF16)], axis=0)],
                                 axis=1)
            md = _mm_tn(lb, rb)
            idx = ch["c"] * n_pair + j
            mb_s[idx] = md[:, :LANES].astype(BF16)
            dl_s[idx] = jnp.where(bd, md[:, LANES:], 0.0)
            wcol_s[idx] = jnp.broadcast_to(ch["wc"][:, sl], (LANES, LANES)).T
        return carry

    lax.fori_loop(0, tb // (RWKV_TRIP * C), phase_a, 0)

    def phase_b(c, carry):
        rs = pl.ds(pl.multiple_of(c * C, C), C)
        sts = [st_ref[j] for j in range(n_pair)]
        stb = [st.astype(BF16) for st in sts]
        upd = [_mm(mb_s[c * n_pair + j], stb[j]) for j in range(n_pair)]
        for j in range(n_pair):
            y_s[rs, sls[j]] += _mm(r2_s[rs, sls[j]], stb[j])
        for j in range(n_pair):
            idx = c * n_pair + j
            st_ref[j] = wcol_s[idx] * sts[j] + upd[j] + dl_s[idx]
        return carry

    lax.fori_loop(0, tb // C, phase_b, 0, unroll=True)

    y = y_s[...]
    inv = 1.0 / RWKV_HEAD
    mean = head_sum(y) * inv
    dlt = y - mean
    var = head_sum(dlt * dlt) * inv
    yn = dlt * lax.rsqrt(var + RWKV_GN_EPS) * lnw_ref[...] + lnb_ref[...]
    o_ref[...] = (yn + bon_s[...]) * g_s[...]


def _rwkv_mix(x, g, w, mu, w0, lora, a0, gup, kk, ka, rk, lnw, lnb):
    B, L, D = x.shape
    tb = min(L, 512)
    ones = jnp.kron(jnp.eye(LANES // RWKV_HEAD, dtype=F32), jnp.ones((RWKV_HEAD, RWKV_HEAD), F32)).astype(BF16)
    tri = jnp.tril(jnp.ones((CHUNK, CHUNK), F32)).astype(BF16)
    consts = (g, w, mu, w0, lora, a0, gup, kk, ka, rk, lnw, lnb, ones, tri)
    act = lambda: pltpu.VMEM((tb, MIX_W), F32)
    n_mat = (tb // CHUNK) * (RWKV_HEADS // 2)
    mat = lambda dt: pltpu.VMEM((n_mat, LANES, LANES), dt)
    return pl.pallas_call(
        _rwkv_kernel,
        grid=(B, L // tb),
        in_specs=[pl.BlockSpec((None, tb, D), lambda b, i: (b, i, 0))] + [_full(c.shape) for c in consts],
        out_specs=pl.BlockSpec((None, tb, MIX_W), lambda b, i: (b, i, 0)),
        out_shape=jax.ShapeDtypeStruct((B, L, MIX_W), F32),
        scratch_shapes=[pltpu.VMEM((8, RWKV_COLS), F32),
                        pltpu.VMEM((RWKV_HEADS // 2, LANES, LANES), F32)] + [act() for _ in range(10)]
        + [mat(BF16), mat(F32), mat(F32)],
        compiler_params=_params(("parallel", "arbitrary")),
        name="rwkv_mix",
    )(x, *consts)


def _gla_kernel(x_ref, g_ref, w_ref, gkup_ref, gkb_ref, ng_ref, tri_ref, o_ref,
                st_ref, q_s, k_s, v_s, gk_s, o_s):
    tb = x_ref.shape[0]
    C = CHUNK

    @pl.when(pl.program_id(1) == 0)
    def _():
        st_ref[...] = jnp.zeros_like(st_ref)

    p = _mm(_rms(x_ref[...], g_ref[...]), w_ref[...])
    q_s[...] = p[:, 0:GLA_QK] * (GLA_DK ** -0.5)
    k_s[...] = p[:, GLA_QK:2 * GLA_QK]
    v_s[...] = p[:, 2 * GLA_QK:2 * GLA_QK + MIX_W]
    go = p[:, 2 * GLA_QK + MIX_W:2 * GLA_QK + 2 * MIX_W]
    z = _mm(p[:, 2 * GLA_QK + 2 * MIX_W:], gkup_ref[...]) + gkb_ref[...]
    gk_s[...] = (jnp.minimum(z, 0.0) - jnp.log(1.0 + jnp.exp(-jnp.abs(z)))) * (1.0 / GLA_GATE_NORM)

    m0 = _iota((C, LANES), 1) < GLA_DK
    mv = _iota((C, 2 * GLA_DV), 1) < GLA_DV
    low = (_iota((2 * C, C), 0) % C) >= _iota((2 * C, C), 1)
    bd = (_iota((LANES, 2 * GLA_DV), 0) >= GLA_DK) == (_iota((LANES, 2 * GLA_DV), 1) >= GLA_DV)

    n_pair = GLA_HEADS // 2
    sls = [slice(LANES * j, LANES * (j + 1)) for j in range(n_pair)]

    def decayed(c):
        rs = pl.ds(pl.multiple_of(c * C, C), C)
        bcum = _mm_split_r(tri_ref[...], gk_s[rs, :])
        blast = bcum[C - 1:C, :]
        kc = k_s[rs, :]
        return dict(rs=rs, qd=q_s[rs, :] * jnp.exp(bcum), ki=kc * jnp.exp(-bcum),
                    ks=kc * jnp.exp(blast - bcum), el=jnp.exp(blast), vc=v_s[rs, :])

    def chunk_group(i, carry):
        chunks = [decayed(GLA_TRIP * i + q) for q in range(GLA_TRIP)]
        items = [(ch, j) for ch in chunks for j in range(n_pair)]
        vps = [ch["vc"][:, 2 * GLA_DV * j:2 * GLA_DV * (j + 1)].astype(BF16) for ch, j in items]
        atts = []
        for ch, j in items:
            qp = ch["qd"][:, sls[j]]
            l2 = jnp.concatenate([jnp.where(m0, qp, 0.0), jnp.where(m0, 0.0, qp)], axis=0)
            atts.append(jnp.where(low, _mm_nt(l2, ch["ki"][:, sls[j]]), 0.0))
        dss = [jnp.where(bd, _mm_tn(ch["ks"][:, sls[j]], vp), 0.0) for (ch, j), vp in zip(items, vps)]
        o2s = [_mm(att, vp) for att, vp in zip(atts, vps)]
        sts = [st_ref[j] for j in range(n_pair)]
        for n, (ch, j) in enumerate(items):
            sv = slice(2 * GLA_DV * j, 2 * GLA_DV * (j + 1))
            o_s[ch["rs"], sv] = jnp.where(mv, o2s[n][:C], o2s[n][C:]) + _mm(ch["qd"][:, sls[j]], sts[j])
            ecol = jnp.broadcast_to(ch["el"][:, sls[j]], (LANES, LANES)).T
            sts[j] = jnp.concatenate([ecol, ecol], axis=1) * sts[j] + dss[n]
        for j in range(n_pair):
            st_ref[j] = sts[j]
        return carry

    lax.fori_loop(0, tb // (GLA_TRIP * C), chunk_group, 0, unroll=True)

    o = o_s[...]
    outs = []
    for h in range(GLA_HEADS):
        oh = o[:, GLA_DV * h:GLA_DV * (h + 1)]
        outs.append(oh * lax.rsqrt(jnp.mean(oh * oh, axis=-1, keepdims=True) + 1e-5) * ng_ref[...])
    o_ref[...] = jnp.concatenate(outs, axis=1) * (go * _sigmoid(go))


def _gla_mix(x, g, w, gkup, gkb, ng):
    B, L, D = x.shape
    tb = min(L, 512)
    tri = jnp.tril(jnp.ones((CHUNK, CHUNK), F32)).astype(BF16)
    consts = (g, w, gkup, gkb, ng, tri)
    return pl.pallas_call(
        _gla_kernel,
        grid=(B, L // tb),
        in_specs=[pl.BlockSpec((None, tb, D), lambda b, i: (b, i, 0))] + [_full(c.shape) for c in consts],
        out_specs=pl.BlockSpec((None, tb, MIX_W), lambda b, i: (b, i, 0)),
        out_shape=jax.ShapeDtypeStruct((B, L, MIX_W), F32),
        scratch_shapes=[pltpu.VMEM((GLA_HEADS // 2, LANES, 2 * GLA_DV), F32),
                        pltpu.VMEM((tb, GLA_QK), F32), pltpu.VMEM((tb, GLA_QK), F32),
                        pltpu.VMEM((tb, MIX_W), F32), pltpu.VMEM((tb, GLA_QK), F32),
                        pltpu.VMEM((tb, MIX_W), F32)],
        compiler_params=_params(("parallel", "arbitrary")),
        name="gla_mix",
    )(x, *consts)


def _merge_kernel(x_ref, yrw_ref, ys5_ref, ygla_ref, g_ref, wg_ref, wb_ref, wo_ref, o_ref):
    x = x_ref[...]
    gates = _sigmoid(_mm(_rms(x, g_ref[...]), wg_ref[...]))
    merged = (gates[:, 0:D_MODEL] * _mm(yrw_ref[...], wb_ref[0])
              + gates[:, D_MODEL:2 * D_MODEL] * _mm(ys5_ref[...], wb_ref[1])
              + gates[:, 2 * D_MODEL:] * _mm(ygla_ref[...], wb_ref[2]))
    o_ref[...] = x + _mm(merged, wo_ref[...])


def _merge(x, y_rw, y_s5, y_gla, g, wg, wb, wo):
    B, L, D = x.shape
    tm = min(L, 1024)
    bspec = lambda w: pl.BlockSpec((None, tm, w), lambda b, i: (b, i, 0))
    return pl.pallas_call(
        _merge_kernel,
        grid=(B, L // tm),
        in_specs=[bspec(D), bspec(MIX_W), bspec(MIX_W), bspec(MIX_W),
                  _full(g.shape), _resident(wg.shape), _resident(wb.shape), _resident(wo.shape)],
        out_specs=bspec(D),
        out_shape=jax.ShapeDtypeStruct((B, L, D), F32),
        compiler_params=_params(("parallel", "parallel")),
        name="merge",
    )(x, y_rw, y_s5, y_gla, g, wg, wb, wo)


def _xattn_kernel(x_ref, kv_ref, g_ref, wq_ref, wo_ref, o_ref):
    x = x_ref[...]
    q = _mm(_rms(x, g_ref[...]), wq_ref[...])
    hs = [slice(XA_HEAD * h, XA_HEAD * (h + 1)) for h in range(XA_HEADS)]
    ss = [_mm_nt(q[:, sl], kv_ref[:, sl]) * (XA_HEAD ** -0.5) for sl in hs]
    es = [jnp.exp(s - jnp.max(s, axis=-1, keepdims=True)) for s in ss]
    pvs = [_mm(e, kv_ref[:, D_MODEL + XA_HEAD * h:D_MODEL + XA_HEAD * (h + 1)]) for h, e in enumerate(es)]
    outs = [pv / jnp.sum(e, axis=-1, keepdims=True) for pv, e in zip(pvs, es)]
    o_ref[...] = x + _mm(jnp.concatenate(outs, axis=1), wo_ref[...])


def _xattn(x, kv, g, wq, wo):
    B, L, D = x.shape
    M = kv.shape[1]
    tq = min(L, 1024)
    xspec = pl.BlockSpec((None, tq, D), lambda b, i: (b, i, 0))
    return pl.pallas_call(
        _xattn_kernel,
        grid=(B, L // tq),
        in_specs=[xspec, pl.BlockSpec((None, M, 2 * D), lambda b, i: (b, 0, 0)),
                  _full(g.shape), _resident(wq.shape), _resident(wo.shape)],
        out_specs=xspec,
        out_shape=jax.ShapeDtypeStruct((B, L, D), F32),
        compiler_params=_params(("parallel", "parallel")),
        name="xattn",
    )(x, kv, g, wq, wo)


FFN_HALO = 16


def _ffn_kernel(final_norm, x_ref, xh_ref, g_ref, gf_ref, wug_ref, wuv_ref, cw_ref, cb_ref, wd_ref,
                o_ref, hn_s, ug_a, uv_a, ug_b, uv_b):
    nj = wug_ref.shape[0]
    tm = x_ref.shape[0]
    halo = _rms(xh_ref[...], g_ref[...])
    halo = jnp.where(pl.program_id(1) == 0, 0.0, halo)
    hn_s[0:FFN_HALO, :] = halo.astype(BF16)
    hn_s[FFN_HALO:, :] = _rms(x_ref[...], g_ref[...]).astype(BF16)
    o_ref[...] = x_ref[...]

    def up(jb, wr_g, wr_v):
        hn = hn_s[...]
        wr_g[...] = jnp.dot(hn, wug_ref[jb], preferred_element_type=F32)
        wr_v[...] = jnp.dot(hn, wuv_ref[jb], preferred_element_type=F32)

    def conv(rd, jb):
        cw = cw_ref[jb]
        tap = lambda d: rd[pl.ds(FFN_HALO - d, tm), :]
        return cw[2:3, :] * tap(0) + cw[1:2, :] * tap(1) + cw[0:1, :] * tap(2) + cb_ref[jb]

    def down(jb, rd_g, rd_v):
        gate = conv(rd_g, jb)
        act = gate * _sigmoid(gate) * conv(rd_v, nj + jb)
        o_ref[...] += _mm(act, wd_ref[jb])

    up(0, ug_a, uv_a)

    def two_blocks(t, carry):
        up(2 * t + 1, ug_b, uv_b)
        down(2 * t, ug_a, uv_a)
        up(2 * t + 2, ug_a, uv_a)
        down(2 * t + 1, ug_b, uv_b)
        return carry

    lax.fori_loop(0, (nj - 1) // 2, two_blocks, 0)
    down(nj - 1, ug_a, uv_a)
    if final_norm:
        o_ref[...] = _rms(o_ref[...], gf_ref[...])


def _ffn(x, g, w_up, conv_w, conv_b, w_down, g_final, final_norm):
    B, L, D = x.shape
    tm = min(L, 1024)
    bn = 256
    nj = D_FF // bn
    assert nj % 2 == 1 and nj * bn == D_FF
    hb = tm // FFN_HALO
    wu = w_up.reshape(D, 2 * nj, bn).transpose(1, 0, 2)
    cw = conv_w.reshape(3, 2 * nj, bn).transpose(1, 0, 2)
    cb = conv_b.reshape(2 * nj, 1, bn)
    wd = w_down.reshape(nj, bn, D)
    xspec = pl.BlockSpec((None, tm, D), lambda b, i: (b, i, 0))
    resident = lambda a: pl.BlockSpec(a.shape, lambda b, i: (0,) * a.ndim, pipeline_mode=pl.Buffered(1))
    ubuf = lambda: pltpu.VMEM((FFN_HALO + tm, bn), F32)
    return pl.pallas_call(
        functools.partial(_ffn_kernel, final_norm),
        grid=(B, L // tm),
        in_specs=[xspec,
                  pl.BlockSpec((None, FFN_HALO, D), lambda b, i: (b, jnp.maximum(i * hb - 1, 0), 0)),
                  _full(g.shape), _full(g_final.shape),
                  pl.BlockSpec((nj, D, bn), lambda b, i: (0, 0, 0), pipeline_mode=pl.Buffered(1)),
                  pl.BlockSpec((nj, D, bn), lambda b, i: (1, 0, 0), pipeline_mode=pl.Buffered(1)),
                  resident(cw), resident(cb), resident(wd)],
        out_specs=xspec,
        out_shape=jax.ShapeDtypeStruct((B, L, D), F32),
        scratch_shapes=[pltpu.VMEM((FFN_HALO + tm, D), BF16), ubuf(), ubuf(), ubuf(), ubuf()],
        compiler_params=_params(("parallel", "parallel")),
        name="conv_ffn",
    )(x, x, g, g_final, wu, wu, cw, cb, wd)


def _row(v):
    return v.reshape(1, -1)


def kernel(x, mem, norm_mix, w_in, rw_mu, rw_w0, rw_w_up, rw_a0, rw_a_up, rw_g_up, rw_k_k, rw_k_a, rw_r_k, rw_ln_w, rw_ln_b, s5_a_re, s5_a_im, s5_b_re, s5_b_im, s5_c_re, s5_c_im, s5_d, s5_log_step, s5_w_glu, s5_b_glu, gla_gk_up, gla_gk_b, gla_norm, w_branch, w_out, norm_mem, norm_xattn, xa_wq, xa_wkv, xa_wo, norm_ffn, ffn_w_up, ffn_conv, ffn_conv_b, ffn_w_down, norm_final):
    B, L, D = x.shape
    o_s5 = RWKV_COLS
    o_gla = o_s5 + S5_COLS
    o_gate = o_gla + GLA_COLS
    for l in range(DEPTH):
        wl = w_in[l]
        g_mix = _row(norm_mix[l])
        tabs = _s5_tables(s5_a_re[l], s5_a_im[l], s5_b_re[l], s5_b_im[l], s5_c_re[l], s5_c_im[l],
                          s5_log_step[l])
        y_s5 = _s5_mix(x, g_mix, wl[:, o_s5:o_gla].astype(BF16), *tabs, _row(s5_d[l]),
                       s5_w_glu[l].astype(BF16), _row(s5_b_glu[l]))
        zeros = jnp.zeros((RWKV_DECAY_RANK, MIX_W), w_in.dtype)
        lora = jnp.concatenate([jnp.concatenate([rw_w_up[l], zeros], axis=1),
                                jnp.concatenate([zeros, rw_a_up[l]], axis=1)], axis=0).astype(BF16)
        y_rw = _rwkv_mix(x, g_mix, wl[:, :o_s5].astype(BF16), _row(rw_mu[l]), _row(rw_w0[l]), lora,
                         _row(rw_a0[l]), rw_g_up[l].astype(BF16), _row(rw_k_k[l]), _row(rw_k_a[l]),
                         _row(rw_r_k[l]), _row(rw_ln_w[l]), _row(rw_ln_b[l]))
        w_gla = wl[:, o_gla:o_gate]
        w_gla = jnp.concatenate(
            [w_gla[:, :2 * GLA_QK + MIX_W], w_gla[:, 2 * GLA_QK + MIX_W + GLA_GATE_RANK:],
             w_gla[:, 2 * GLA_QK + MIX_W:2 * GLA_QK + MIX_W + GLA_GATE_RANK],
             jnp.zeros((D, GLA_COLS_PAD - GLA_COLS), w_in.dtype)], axis=1).astype(BF16)
        gkup = jnp.concatenate([gla_gk_up[l], jnp.zeros((LANES - GLA_GATE_RANK, GLA_QK), w_in.dtype)],
                               axis=0).astype(BF16)
        y_gla = _gla_mix(x, g_mix, w_gla, gkup, _row(gla_gk_b[l]), _row(gla_norm[l]))
        x = _merge(x, y_rw, y_s5, y_gla, g_mix, wl[:, o_gate:].astype(BF16),
                   w_branch[l].astype(BF16), w_out[l].astype(BF16))
        kv = _kv_proj(mem, _row(norm_mem), xa_wkv[l].astype(BF16))
        x = _xattn(x, kv, _row(norm_xattn[l]), xa_wq[l].astype(BF16), xa_wo[l].astype(BF16))
        x = _ffn(x, _row(norm_ffn[l]), ffn_w_up[l].astype(BF16), ffn_conv[l], _row(ffn_conv_b[l]),
                 ffn_w_down[l].astype(BF16), _row(norm_final), l == DEPTH - 1)
    return x
```

```python
import functools
import math

import jax
import jax.numpy as jnp
from jax import lax
from jax.experimental import pallas as pl
from jax.experimental.pallas import tpu as pltpu

F32 = jnp.float32
BF16 = jnp.bfloat16

D_MODEL = 1024
DEPTH = 4
N_BRANCH = 3
MIX_W = 512

RWKV_HEAD = 64
RWKV_HEADS = 8
RWKV_DECAY_RANK = 64
RWKV_A_RANK = 64
RWKV_G_RANK = 128
RWKV_GN_EPS = RWKV_HEAD * 1e-5
RWKV_COLS = 3 * MIX_W + RWKV_DECAY_RANK + RWKV_A_RANK + RWKV_G_RANK

S5_GROUP = 16
S5_GROUPS = 32
S5_STATE = 64
S5_COLS = MIX_W
S5_NS = S5_GROUPS * S5_STATE

GLA_HEADS = 4
GLA_DV = 128
GLA_DK = 64
GLA_GATE_RANK = 16
GLA_GATE_NORM = 16.0
GLA_QK = GLA_HEADS * GLA_DK
GLA_COLS = 2 * GLA_QK + 2 * MIX_W + GLA_GATE_RANK
GLA_COLS_PAD = 1664

XA_HEADS = 4
XA_HEAD = 256
D_FF = 2816
NORM_EPS = 1e-6

LANES = 128
CHUNK = 64
RWKV_TRIP = 4
GLA_TRIP = 4
EXP_M05 = math.exp(-0.5)
GELU_C = math.sqrt(2.0 / math.pi)

VMEM_LIMIT = 56 * 1024 * 1024


def _mm(a, b):
    return jnp.dot(a.astype(BF16), b.astype(BF16), preferred_element_type=F32)


def _mm_nt(a, b):
    return lax.dot_general(a.astype(BF16), b.astype(BF16), (((1,), (1,)), ((), ())),
                           preferred_element_type=F32)


def _mm_tn(a, b):
    return lax.dot_general(a.astype(BF16), b.astype(BF16), (((0,), (0,)), ((), ())),
                           preferred_element_type=F32)


def _split(a):
    hi = a.astype(BF16)
    lo = (a - hi.astype(F32)).astype(BF16)
    return hi, lo


def _mm_split_l(a, b_exact):
    hi, lo = _split(a)
    return (jnp.dot(hi, b_exact, preferred_element_type=F32)
            + jnp.dot(lo, b_exact, preferred_element_type=F32))


def _mm_split_r(a_exact, b):
    hi, lo = _split(b)
    return (jnp.dot(a_exact, hi, preferred_element_type=F32)
            + jnp.dot(a_exact, lo, preferred_element_type=F32))


def _sigmoid(x):
    return 1.0 / (1.0 + jnp.exp(-x))


def _rms(x, g):
    ms = jnp.mean(x * x, axis=-1, keepdims=True)
    return x * lax.rsqrt(ms + NORM_EPS) * g


def _iota(shape, dim):
    return lax.broadcasted_iota(jnp.int32, shape, dim)


def _params(sem):
    return pltpu.CompilerParams(dimension_semantics=sem, vmem_limit_bytes=VMEM_LIMIT)


def _full(shape):
    nd = len(shape)
    return pl.BlockSpec(shape, lambda *_: (0,) * nd)


def _resident(shape):
    nd = len(shape)
    return pl.BlockSpec(shape, lambda *_: (0,) * nd, pipeline_mode=pl.Buffered(1))


def _norm_mm_kernel(x_ref, g_ref, w_ref, o_ref):
    o_ref[...] = _mm(_rms(x_ref[...], g_ref[...]), w_ref[...]).astype(o_ref.dtype)


def _kv_proj(mem, g, w):
    B, M, D = mem.shape
    N = w.shape[1]
    return pl.pallas_call(
        _norm_mm_kernel,
        grid=(B,),
        in_specs=[pl.BlockSpec((None, M, D), lambda b: (b, 0, 0)),
                  _full((1, D)), _full((D, N))],
        out_specs=pl.BlockSpec((None, M, N), lambda b: (b, 0, 0)),
        out_shape=jax.ShapeDtypeStruct((B, M, N), BF16),
        compiler_params=_params(("parallel",)),
        name="kv_proj",
    )(mem, g, w)


def _s5_kernel(x_ref, g_ref, w_ref, are_ref, aim_ref, bre_ref, bim_ref, cre_ref, cim_ref, d_ref,
               wglu_ref, bglu_ref, o_ref, sre, sim, st_re, st_im):
    nb, tt, D = x_ref.shape
    rows = nb * tt

    @pl.when(pl.program_id(0) == 0)
    def _():
        st_re[...] = jnp.zeros_like(st_re)
        st_im[...] = jnp.zeros_like(st_im)

    p = _mm(_rms(x_ref[...].reshape(rows, D), g_ref[...]), w_ref[...])
    u = pltpu.einshape("btc->tbc", p.reshape(nb, tt, S5_COLS)).reshape(rows, S5_COLS)
    half = S5_NS // 2
    for kt in range(2):
        uk = u[:, 256 * kt:256 * (kt + 1)].astype(BF16)
        sre[:, half * kt:half * (kt + 1)] = jnp.dot(uk, bre_ref[kt], preferred_element_type=F32)
        sim[:, half * kt:half * (kt + 1)] = jnp.dot(uk, bim_ref[kt], preferred_element_type=F32)

    cw = 512
    for cg in range(S5_NS // cw):
        cs = slice(cw * cg, cw * (cg + 1))
        ar = jnp.broadcast_to(are_ref[:, cs], (nb, cw))
        ai = jnp.broadcast_to(aim_ref[:, cs], (nb, cw))

        def body(t, carry):
            re, im = carry
            rs = pl.ds(pl.multiple_of(t * nb, nb), nb)
            nre = ar * re - ai * im + sre[rs, cs]
            nim = ar * im + ai * re + sim[rs, cs]
            sre[rs, cs] = nre
            sim[rs, cs] = nim
            return nre, nim

        re, im = lax.fori_loop(0, tt, body, (st_re[:, cs], st_im[:, cs]), unroll=4)
        st_re[:, cs] = re
        st_im[:, cs] = im

    ys = []
    for jt in range(2):
        hs = slice(half * jt, half * (jt + 1))
        ys.append(_mm(sre[:, hs], cre_ref[jt]) + _mm(sim[:, hs], cim_ref[jt]))
    y = jnp.concatenate(ys, axis=1) + d_ref[...] * u
    y = 0.5 * y * (1.0 + jnp.tanh(GELU_C * (y + 0.044715 * (y * y * y))))
    y = y * _sigmoid(_mm(y, wglu_ref[...]) + bglu_ref[...])
    o_ref[...] = pltpu.einshape("tbc->btc", y.reshape(tt, nb, S5_COLS))


def _s5_mix(x, g, w, are, aim, bre, bim, cre, cim, d, wglu, bglu):
    B, L, D = x.shape
    tt = min(L, 64)
    rows = tt * B
    consts = (g, w, are, aim, bre, bim, cre, cim, d, wglu, bglu)
    return pl.pallas_call(
        _s5_kernel,
        grid=(L // tt,),
        in_specs=[pl.BlockSpec((B, tt, D), lambda i: (0, i, 0))] + [_full(c.shape) for c in consts],
        out_specs=pl.BlockSpec((B, tt, S5_COLS), lambda i: (0, i, 0)),
        out_shape=jax.ShapeDtypeStruct((B, L, S5_COLS), F32),
        scratch_shapes=[pltpu.VMEM((rows, S5_NS), F32), pltpu.VMEM((rows, S5_NS), F32),
                        pltpu.VMEM((B, S5_NS), F32), pltpu.VMEM((B, S5_NS), F32)],
        compiler_params=_params(("arbitrary",)),
        name="s5_mix",
    )(x, *consts)


def _s5_tables(a_re, a_im, b_re, b_im, c_re, c_im, log_step):
    dt = jnp.exp(log_step)[:, None]
    mag = jnp.exp(dt * a_re)
    ang = dt * a_im
    abar_re, abar_im = mag * jnp.cos(ang), mag * jnp.sin(ang)
    den = a_re * a_re + a_im * a_im
    nr, ni = abar_re - 1.0, abar_im
    f_re = ((nr * a_re + ni * a_im) / den)[..., None]
    f_im = ((ni * a_re - nr * a_im) / den)[..., None]
    bbar_re = f_re * b_re - f_im * b_im
    bbar_im = f_re * b_im + f_im * b_re
    gl = S5_GROUPS // 2
    eye = jnp.eye(gl, dtype=a_re.dtype)

    def bmat(bb):
        bb = bb.reshape(2, gl, S5_STATE, S5_GROUP)
        return jnp.einsum('kgnc,gh->kgchn', bb, eye).reshape(2, gl * S5_GROUP, gl * S5_STATE)

    def cmat(cc):
        cc = cc.reshape(2, gl, S5_GROUP, S5_STATE)
        return jnp.einsum('kgcn,gh->kgnhc', cc, eye).reshape(2, gl * S5_STATE, gl * S5_GROUP)

    return (abar_re.reshape(1, S5_NS), abar_im.reshape(1, S5_NS),
            bmat(bbar_re).astype(BF16), bmat(bbar_im).astype(BF16),
            cmat(c_re).astype(BF16), cmat(-c_im).astype(BF16))


def _rwkv_kernel(x_ref, g_ref, w_ref, mu_ref, w0_ref, lora_ref, a0_ref, gup_ref, kk_ref, ka_ref,
                 rk_ref, lnw_ref, lnb_ref, ones_ref, tri_ref, o_ref,
                 prev_ref, st_ref, ld_s, al_s, be_s, k_s, r_s, v_s, y_s, bon_s, g_s, r2_s,
                 mb_s, dl_s, wcol_s):
    tb = x_ref.shape[0]
    C = CHUNK

    @pl.when(pl.program_id(1) == 0)
    def _():
        prev_ref[...] = jnp.zeros_like(prev_ref)
        st_ref[...] = jnp.zeros_like(st_ref)

    p = _mm(_rms(x_ref[...], g_ref[...]), w_ref[...])
    shifted = jnp.where(_iota(p.shape, 0) == 0, prev_ref[0:1, :], pltpu.roll(p, 1, axis=0))
    prev_ref[0:1, :] = p[tb - 1:tb, :]
    p = p + (shifted - p) * mu_ref[...]

    r = p[:, 0:MIX_W]
    k = p[:, MIX_W:2 * MIX_W]
    v = p[:, 2 * MIX_W:3 * MIX_W]
    lin = p[:, 3 * MIX_W:3 * MIX_W + LANES]
    gd = p[:, 3 * MIX_W + LANES:]
    lin = jnp.where(_iota(lin.shape, 1) < RWKV_DECAY_RANK, jnp.tanh(lin), lin)
    proj = _mm(lin, lora_ref[...])
    ld_s[...] = -EXP_M05 * _sigmoid(w0_ref[...] + proj[:, :MIX_W])
    a = _sigmoid(a0_ref[...] + proj[:, MIX_W:])
    g_s[...] = _mm(_sigmoid(gd), gup_ref[...])
    kkv = k * kk_ref[...]
    def head_sum(t):
        return jnp.concatenate([_mm(t[:, LANES * i:LANES * (i + 1)], ones_ref[...])
                                for i in range(MIX_W // LANES)], axis=1)

    kk = kkv * lax.rsqrt(head_sum(kkv * kkv) + 1e-12)
    k2 = k * (1.0 + (a - 1.0) * ka_ref[...])
    bon_s[...] = head_sum(r * k2 * rk_ref[...]) * v
    al_s[...] = -kk
    be_s[...] = kk * a
    k_s[...] = k2
    r_s[...] = r
    v_s[...] = v

    m0 = _iota((C, LANES), 1) < RWKV_HEAD
    ri = _iota((2 * C, 2 * C), 0)
    ci = _iota((2 * C, 2 * C), 1)
    same = (ri >= C) == (ci >= C)
    strict = same & (ri > ci)
    incl = same & (ri >= ci)
    top = ri < C
    eye = (ri == ci).astype(F32)
    bd = (_iota((LANES, LANES), 0) >= RWKV_HEAD) == (_iota((LANES, LANES), 1) >= RWKV_HEAD)
    zc = jnp.zeros((C, LANES), F32)
    z2 = jnp.zeros((2 * C, LANES), BF16)
    n_lvl = C.bit_length() - 1
    n_pair = RWKV_HEADS // 2
    sls = [slice(LANES * j, LANES * (j + 1)) for j in range(n_pair)]

    def decayed(c):
        rs = pl.ds(pl.multiple_of(c * C, C), C)
        ldc = ld_s[rs, :]
        cum = _mm_split_r(tri_ref[...], ldc)
        tot = cum[C - 1:C, :]
        e_neg = jnp.exp(-cum)
        e_st = jnp.exp(tot - cum)
        be = be_s[rs, :]
        kc = k_s[rs, :]
        return dict(rs=rs, c=c, al=al_s[rs, :] * jnp.exp(cum - ldc), bt=be * e_neg, bs=be * e_st,
                    kt=kc * e_neg, ks=kc * e_st, rt=r_s[rs, :] * jnp.exp(cum), vc=v_s[rs, :],
                    wc=jnp.exp(tot))

    def phase_a(i, carry):
        chunks = [decayed(RWKV_TRIP * i + q) for q in range(RWKV_TRIP)]
        items = [(ch, j) for ch in chunks for j in range(n_pair)]
        l4s, gms = [], []
        for ch, j in items:
            sl = sls[j]
            ap, rp = ch["al"][:, sl], ch["rt"][:, sl]
            l4 = jnp.concatenate([jnp.where(m0, ap, 0.0), jnp.where(m0, 0.0, ap),
                                  jnp.where(m0, rp, 0.0), jnp.where(m0, 0.0, rp)], axis=0).astype(BF16)
            r2 = jnp.concatenate([ch["bt"][:, sl], ch["kt"][:, sl]], axis=0)
            l4s.append(l4)
            gms.append(_mm_nt(l4, r2))
        gas = [gm[:2 * C] for gm in gms]
        grs = [gm[2 * C:] for gm in gms]
        gas_sw = [pltpu.roll(ga, C, axis=1) for ga in gas]
        grs_sw = [pltpu.roll(gr, C, axis=1) for gr in grs]
        v2s = [jnp.concatenate([ch["vc"][:, sls[j]], ch["vc"][:, sls[j]]], axis=0).astype(BF16)
               for ch, j in items]
        w1s = [_mm(jnp.where(strict, jnp.where(top, sw, ga), 0.0), v2) for ga, sw, v2 in zip(gas, gas_sw, v2s)]
        ns = [jnp.where(strict, jnp.where(top, ga, sw), 0.0) for ga, sw in zip(gas, gas_sw)]
        ts = [eye + n for n in ns]
        pws = [n.astype(BF16) for n in ns]
        for _ in range(1, n_lvl):
            pws = [_mm(pw, pw).astype(BF16) for pw in pws]
            ts = [t + _mm(pw, t) for pw, t in zip(pws, ts)]
        xs = [_mm(t, jnp.concatenate([w1.astype(BF16), l4[:2 * C]], axis=1))
              for t, w1, l4 in zip(ts, w1s, l4s)]
        uls = [x[:, :LANES].astype(BF16) for x in xs]
        a2s = [x[:, LANES:].astype(BF16) for x in xs]
        arbs = [jnp.where(incl, jnp.where(top, gr, sw), 0.0).astype(BF16) for gr, sw in zip(grs, grs_sw)]
        arks = [jnp.where(incl, jnp.where(top, sw, gr), 0.0).astype(BF16) for gr, sw in zip(grs, grs_sw)]
        yl2s = [_mm(jnp.concatenate([arb, ark], axis=1), jnp.concatenate([ul, v2], axis=0))
                for arb, ark, ul, v2 in zip(arbs, arks, uls, v2s)]
        ras = [_mm(arb, a2) for arb, a2 in zip(arbs, a2s)]
        for (ch, j), yl2, ra in zip(items, yl2s, ras):
            y_s[ch["rs"], sls[j]] = jnp.where(m0, yl2[:C], yl2[C:])
            r2_s[ch["rs"], sls[j]] = ra[:C] + ra[C:] + ch["rt"][:, sls[j]]
        for (ch, j), ul, a2 in zip(items, uls, a2s):
            sl = sls[j]
            lb = jnp.concatenate([jnp.where(m0, ch["bs"][:, sl], 0.0), jnp.where(m0, 0.0, ch["bs"][:, sl]),
                                  ch["ks"][:, sl], zc], axis=0)
            rb = jnp.concatenate([jnp.concatenate([a2, z2], axis=0),
                                  jnp.concatenate([ul, ch["vc"][:, sl].astype(BF16), zc.astype(BF16)], axis=0)],
                                 axis=1)
            md = _mm_tn(lb, rb)
            idx = ch["c"] * n_pair + j
            mb_s[idx] = md[:, :LANES].astype(BF16)
            dl_s[idx] = jnp.where(bd, md[:, LANES:], 0.0)
            wcol_s[idx] = jnp.broadcast_to(ch["wc"][:, sl], (LANES, LANES)).T
        return carry

    lax.fori_loop(0, tb // (RWKV_TRIP * C), phase_a, 0)

    def phase_b(c, carry):
        rs = pl.ds(pl.multiple_of(c * C, C), C)
        sts = [st_ref[j] for j in range(n_pair)]
        stb = [st.astype(BF16) for st in sts]
        upd = [_mm(mb_s[c * n_pair + j], stb[j]) for j in range(n_pair)]
        for j in range(n_pair):
            y_s[rs, sls[j]] += _mm(r2_s[rs, sls[j]], stb[j])
        for j in range(n_pair):
            idx = c * n_pair + j
            st_ref[j] = wcol_s[idx] * sts[j] + upd[j] + dl_s[idx]
        return carry

    lax.fori_loop(0, tb // C, phase_b, 0, unroll=True)

    y = y_s[...]
    inv = 1.0 / RWKV_HEAD
    mean = head_sum(y) * inv
    dlt = y - mean
    var = head_sum(dlt * dlt) * inv
    yn = dlt * lax.rsqrt(var + RWKV_GN_EPS) * lnw_ref[...] + lnb_ref[...]
    o_ref[...] = (yn + bon_s[...]) * g_s[...]


def _rwkv_mix(x, g, w, mu, w0, lora, a0, gup, kk, ka, rk, lnw, lnb):
    B, L, D = x.shape
    tb = min(L, 512)
    ones = jnp.kron(jnp.eye(LANES // RWKV_HEAD, dtype=F32), jnp.ones((RWKV_HEAD, RWKV_HEAD), F32)).astype(BF16)
    tri = jnp.tril(jnp.ones((CHUNK, CHUNK), F32)).astype(BF16)
    consts = (g, w, mu, w0, lora, a0, gup, kk, ka, rk, lnw, lnb, ones, tri)
    act = lambda: pltpu.VMEM((tb, MIX_W), F32)
    n_mat = (tb // CHUNK) * (RWKV_HEADS // 2)
    mat = lambda dt: pltpu.VMEM((n_mat, LANES, LANES), dt)
    return pl.pallas_call(
        _rwkv_kernel,
        grid=(B, L // tb),
        in_specs=[pl.BlockSpec((None, tb, D), lambda b, i: (b, i, 0))] + [_full(c.shape) for c in consts],
        out_specs=pl.BlockSpec((None, tb, MIX_W), lambda b, i: (b, i, 0)),
        out_shape=jax.ShapeDtypeStruct((B, L, MIX_W), F32),
        scratch_shapes=[pltpu.VMEM((8, RWKV_COLS), F32),
                        pltpu.VMEM((RWKV_HEADS // 2, LANES, LANES), F32)] + [act() for _ in range(10)]
        + [mat(BF16), mat(F32), mat(F32)],
        compiler_params=_params(("parallel", "arbitrary")),
        name="rwkv_mix",
    )(x, *consts)


def _gla_kernel(x_ref, g_ref, w_ref, gkup_ref, gkb_ref, ng_ref, tri_ref, o_ref,
                st_ref, q_s, k_s, v_s, gk_s, o_s):
    tb = x_ref.shape[0]
    C = CHUNK

    @pl.when(pl.program_id(1) == 0)
    def _():
        st_ref[...] = jnp.zeros_like(st_ref)

    p = _mm(_rms(x_ref[...], g_ref[...]), w_ref[...])
    q_s[...] = p[:, 0:GLA_QK] * (GLA_DK ** -0.5)
    k_s[...] = p[:, GLA_QK:2 * GLA_QK]
    v_s[...] = p[:, 2 * GLA_QK:2 * GLA_QK + MIX_W]
    go = p[:, 2 * GLA_QK + MIX_W:2 * GLA_QK + 2 * MIX_W]
    z = _mm(p[:, 2 * GLA_QK + 2 * MIX_W:], gkup_ref[...]) + gkb_ref[...]
    gk_s[...] = (jnp.minimum(z, 0.0) - jnp.log(1.0 + jnp.exp(-jnp.abs(z)))) * (1.0 / GLA_GATE_NORM)

    m0 = _iota((C, LANES), 1) < GLA_DK
    mv = _iota((C, 2 * GLA_DV), 1) < GLA_DV
    low = (_iota((2 * C, C), 0) % C) >= _iota((2 * C, C), 1)
    bd = (_iota((LANES, 2 * GLA_DV), 0) >= GLA_DK) == (_iota((LANES, 2 * GLA_DV), 1) >= GLA_DV)

    n_pair = GLA_HEADS // 2
    sls = [slice(LANES * j, LANES * (j + 1)) for j in range(n_pair)]

    def decayed(c):
        rs = pl.ds(pl.multiple_of(c * C, C), C)
        bcum = _mm_split_r(tri_ref[...], gk_s[rs, :])
        blast = bcum[C - 1:C, :]
        kc = k_s[rs, :]
        return dict(rs=rs, qd=q_s[rs, :] * jnp.exp(bcum), ki=kc * jnp.exp(-bcum),
                    ks=kc * jnp.exp(blast - bcum), el=jnp.exp(blast), vc=v_s[rs, :])

    def chunk_group(i, carry):
        chunks = [decayed(GLA_TRIP * i + q) for q in range(GLA_TRIP)]
        items = [(ch, j) for ch in chunks for j in range(n_pair)]
        vps = [ch["vc"][:, 2 * GLA_DV * j:2 * GLA_DV * (j + 1)].astype(BF16) for ch, j in items]
        atts = []
        for ch, j in items:
            qp = ch["qd"][:, sls[j]]
            l2 = jnp.concatenate([jnp.where(m0, qp, 0.0), jnp.where(m0, 0.0, qp)], axis=0)
            atts.append(jnp.where(low, _mm_nt(l2, ch["ki"][:, sls[j]]), 0.0))
        dss = [jnp.where(bd, _mm_tn(ch["ks"][:, sls[j]], vp), 0.0) for (ch, j), vp in zip(items, vps)]
        o2s = [_mm(att, vp) for att, vp in zip(atts, vps)]
        sts = [st_ref[j] for j in range(n_pair)]
        for n, (ch, j) in enumerate(items):
            sv = slice(2 * GLA_DV * j, 2 * GLA_DV * (j + 1))
            o_s[ch["rs"], sv] = jnp.where(mv, o2s[n][:C], o2s[n][C:]) + _mm(ch["qd"][:, sls[j]], sts[j])
            ecol = jnp.broadcast_to(ch["el"][:, sls[j]], (LANES, LANES)).T
            sts[j] = jnp.concatenate([ecol, ecol], axis=1) * sts[j] + dss[n]
        for j in range(n_pair):
            st_ref[j] = sts[j]
        return carry

    lax.fori_loop(0, tb // (GLA_TRIP * C), chunk_group, 0, unroll=True)

    o = o_s[...]
    outs = []
    for h in range(GLA_HEADS):
        oh = o[:, GLA_DV * h:GLA_DV * (h + 1)]
        outs.append(oh * lax.rsqrt(jnp.mean(oh * oh, axis=-1, keepdims=True) + 1e-5) * ng_ref[...])
    o_ref[...] = jnp.concatenate(outs, axis=1) * (go * _sigmoid(go))


def _gla_mix(x, g, w, gkup, gkb, ng):
    B, L, D = x.shape
    tb = min(L, 1024)
    tri = jnp.tril(jnp.ones((CHUNK, CHUNK), F32)).astype(BF16)
    consts = (g, w, gkup, gkb, ng, tri)
    return pl.pallas_call(
        _gla_kernel,
        grid=(B, L // tb),
        in_specs=[pl.BlockSpec((None, tb, D), lambda b, i: (b, i, 0))] + [_full(c.shape) for c in consts],
        out_specs=pl.BlockSpec((None, tb, MIX_W), lambda b, i: (b, i, 0)),
        out_shape=jax.ShapeDtypeStruct((B, L, MIX_W), F32),
        scratch_shapes=[pltpu.VMEM((GLA_HEADS // 2, LANES, 2 * GLA_DV), F32),
                        pltpu.VMEM((tb, GLA_QK), F32), pltpu.VMEM((tb, GLA_QK), F32),
                        pltpu.VMEM((tb, MIX_W), F32), pltpu.VMEM((tb, GLA_QK), F32),
                        pltpu.VMEM((tb, MIX_W), F32)],
        compiler_params=_params(("parallel", "arbitrary")),
        name="gla_mix",
    )(x, *consts)


def _merge_kernel(x_ref, yrw_ref, ys5_ref, ygla_ref, g_ref, wg_ref, wb_ref, wo_ref, o_ref):
    x = x_ref[...]
    gates = _sigmoid(_mm(_rms(x, g_ref[...]), wg_ref[...]))
    merged = (gates[:, 0:D_MODEL] * _mm(yrw_ref[...], wb_ref[0])
              + gates[:, D_MODEL:2 * D_MODEL] * _mm(ys5_ref[...], wb_ref[1])
              + gates[:, 2 * D_MODEL:] * _mm(ygla_ref[...], wb_ref[2]))
    o_ref[...] = x + _mm(merged, wo_ref[...])


def _merge(x, y_rw, y_s5, y_gla, g, wg, wb, wo):
    B, L, D = x.shape
    tm = min(L, 1024)
    bspec = lambda w: pl.BlockSpec((None, tm, w), lambda b, i: (b, i, 0))
    return pl.pallas_call(
        _merge_kernel,
        grid=(B, L // tm),
        in_specs=[bspec(D), bspec(MIX_W), bspec(MIX_W), bspec(MIX_W),
                  _full(g.shape), _resident(wg.shape), _resident(wb.shape), _resident(wo.shape)],
        out_specs=bspec(D),
        out_shape=jax.ShapeDtypeStruct((B, L, D), F32),
        compiler_params=_params(("parallel", "parallel")),
        name="merge",
    )(x, y_rw, y_s5, y_gla, g, wg, wb, wo)


def _xattn_kernel(x_ref, kv_ref, g_ref, wq_ref, wo_ref, o_ref):
    x = x_ref[...]
    q = _mm(_rms(x, g_ref[...]), wq_ref[...])
    hs = [slice(XA_HEAD * h, XA_HEAD * (h + 1)) for h in range(XA_HEADS)]
    ss = [_mm_nt(q[:, sl], kv_ref[:, sl]) * (XA_HEAD ** -0.5) for sl in hs]
    es = [jnp.exp(s - jnp.max(s, axis=-1, keepdims=True)) for s in ss]
    pvs = [_mm(e, kv_ref[:, D_MODEL + XA_HEAD * h:D_MODEL + XA_HEAD * (h + 1)]) for h, e in enumerate(es)]
    outs = [pv / jnp.sum(e, axis=-1, keepdims=True) for pv, e in zip(pvs, es)]
    o_ref[...] = x + _mm(jnp.concatenate(outs, axis=1), wo_ref[...])


def _xattn(x, kv, g, wq, wo):
    B, L, D = x.shape
    M = kv.shape[1]
    tq = min(L, 1024)
    xspec = pl.BlockSpec((None, tq, D), lambda b, i: (b, i, 0))
    return pl.pallas_call(
        _xattn_kernel,
        grid=(B, L // tq),
        in_specs=[xspec, pl.BlockSpec((None, M, 2 * D), lambda b, i: (b, 0, 0)),
                  _full(g.shape), _resident(wq.shape), _resident(wo.shape)],
        out_specs=xspec,
        out_shape=jax.ShapeDtypeStruct((B, L, D), F32),
        compiler_params=_params(("parallel", "parallel")),
        name="xattn",
    )(x, kv, g, wq, wo)


FFN_HALO = 16


def _ffn_kernel(final_norm, x_ref, xh_ref, g_ref, gf_ref, wug_ref, wuv_ref, cw_ref, cb_ref, wd_ref,
                o_ref, hn_s, ug_a, uv_a, ug_b, uv_b):
    nj = wug_ref.shape[0]
    tm = x_ref.shape[0]
    halo = _rms(xh_ref[...], g_ref[...])
    halo = jnp.where(pl.program_id(1) == 0, 0.0, halo)
    hn_s[0:FFN_HALO, :] = halo.astype(BF16)
    hn_s[FFN_HALO:, :] = _rms(x_ref[...], g_ref[...]).astype(BF16)
    o_ref[...] = x_ref[...]

    def up(jb, wr_g, wr_v):
        hn = hn_s[...]
        wr_g[...] = jnp.dot(hn, wug_ref[jb], preferred_element_type=F32)
        wr_v[...] = jnp.dot(hn, wuv_ref[jb], preferred_element_type=F32)

    def conv(rd, jb):
        cw = cw_ref[jb]
        tap = lambda d: rd[pl.ds(FFN_HALO - d, tm), :]
        return cw[2:3, :] * tap(0) + cw[1:2, :] * tap(1) + cw[0:1, :] * tap(2) + cb_ref[jb]

    def down(jb, rd_g, rd_v):
        gate = conv(rd_g, jb)
        act = gate * _sigmoid(gate) * conv(rd_v, nj + jb)
        o_ref[...] += _mm(act, wd_ref[jb])

    up(0, ug_a, uv_a)

    def two_blocks(t, carry):
        up(2 * t + 1, ug_b, uv_b)
        down(2 * t, ug_a, uv_a)
        up(2 * t + 2, ug_a, uv_a)
        down(2 * t + 1, ug_b, uv_b)
        return carry

    lax.fori_loop(0, (nj - 1) // 2, two_blocks, 0)
    down(nj - 1, ug_a, uv_a)
    if final_norm:
        o_ref[...] = _rms(o_ref[...], gf_ref[...])


def _ffn(x, g, w_up, conv_w, conv_b, w_down, g_final, final_norm):
    B, L, D = x.shape
    tm = min(L, 1024)
    bn = 256
    nj = D_FF // bn
    assert nj % 2 == 1 and nj * bn == D_FF
    hb = tm // FFN_HALO
    wu = w_up.reshape(D, 2 * nj, bn).transpose(1, 0, 2)
    cw = conv_w.reshape(3, 2 * nj, bn).transpose(1, 0, 2)
    cb = conv_b.reshape(2 * nj, 1, bn)
    wd = w_down.reshape(nj, bn, D)
    xspec = pl.BlockSpec((None, tm, D), lambda b, i: (b, i, 0))
    resident = lambda a: pl.BlockSpec(a.shape, lambda b, i: (0,) * a.ndim, pipeline_mode=pl.Buffered(1))
    ubuf = lambda: pltpu.VMEM((FFN_HALO + tm, bn), F32)
    return pl.pallas_call(
        functools.partial(_ffn_kernel, final_norm),
        grid=(B, L // tm),
        in_specs=[xspec,
                  pl.BlockSpec((None, FFN_HALO, D), lambda b, i: (b, jnp.maximum(i * hb - 1, 0), 0)),
                  _full(g.shape), _full(g_final.shape),
                  pl.BlockSpec((nj, D, bn), lambda b, i: (0, 0, 0), pipeline_mode=pl.Buffered(1)),
                  pl.BlockSpec((nj, D, bn), lambda b, i: (1, 0, 0), pipeline_mode=pl.Buffered(1)),
                  resident(cw), resident(cb), resident(wd)],
        out_specs=xspec,
        out_shape=jax.ShapeDtypeStruct((B, L, D), F32),
        scratch_shapes=[pltpu.VMEM((FFN_HALO + tm, D), BF16), ubuf(), ubuf(), ubuf(), ubuf()],
        compiler_params=_params(("parallel", "parallel")),
        name="conv_ffn",
    )(x, x, g, g_final, wu, wu, cw, cb, wd)


def _row(v):
    return v.reshape(1, -1)


def kernel(x, mem, norm_mix, w_in, rw_mu, rw_w0, rw_w_up, rw_a0, rw_a_up, rw_g_up, rw_k_k, rw_k_a, rw_r_k, rw_ln_w, rw_ln_b, s5_a_re, s5_a_im, s5_b_re, s5_b_im, s5_c_re, s5_c_im, s5_d, s5_log_step, s5_w_glu, s5_b_glu, gla_gk_up, gla_gk_b, gla_norm, w_branch, w_out, norm_mem, norm_xattn, xa_wq, xa_wkv, xa_wo, norm_ffn, ffn_w_up, ffn_conv, ffn_conv_b, ffn_w_down, norm_final):
    B, L, D = x.shape
    o_s5 = RWKV_COLS
    o_gla = o_s5 + S5_COLS
    o_gate = o_gla + GLA_COLS
    for l in range(DEPTH):
        wl = w_in[l]
        g_mix = _row(norm_mix[l])
        tabs = _s5_tables(s5_a_re[l], s5_a_im[l], s5_b_re[l], s5_b_im[l], s5_c_re[l], s5_c_im[l],
                          s5_log_step[l])
        y_s5 = _s5_mix(x, g_mix, wl[:, o_s5:o_gla].astype(BF16), *tabs, _row(s5_d[l]),
                       s5_w_glu[l].astype(BF16), _row(s5_b_glu[l]))
        zeros = jnp.zeros((RWKV_DECAY_RANK, MIX_W), w_in.dtype)
        lora = jnp.concatenate([jnp.concatenate([rw_w_up[l], zeros], axis=1),
                                jnp.concatenate([zeros, rw_a_up[l]], axis=1)], axis=0).astype(BF16)
        y_rw = _rwkv_mix(x, g_mix, wl[:, :o_s5].astype(BF16), _row(rw_mu[l]), _row(rw_w0[l]), lora,
                         _row(rw_a0[l]), rw_g_up[l].astype(BF16), _row(rw_k_k[l]), _row(rw_k_a[l]),
                         _row(rw_r_k[l]), _row(rw_ln_w[l]), _row(rw_ln_b[l]))
        w_gla = wl[:, o_gla:o_gate]
        w_gla = jnp.concatenate(
            [w_gla[:, :2 * GLA_QK + MIX_W], w_gla[:, 2 * GLA_QK + MIX_W + GLA_GATE_RANK:],
             w_gla[:, 2 * GLA_QK + MIX_W:2 * GLA_QK + MIX_W + GLA_GATE_RANK],
             jnp.zeros((D, GLA_COLS_PAD - GLA_COLS), w_in.dtype)], axis=1).astype(BF16)
        gkup = jnp.concatenate([gla_gk_up[l], jnp.zeros((LANES - GLA_GATE_RANK, GLA_QK), w_in.dtype)],
                               axis=0).astype(BF16)
        y_gla = _gla_mix(x, g_mix, w_gla, gkup, _row(gla_gk_b[l]), _row(gla_norm[l]))
        x = _merge(x, y_rw, y_s5, y_gla, g_mix, wl[:, o_gate:].astype(BF16),
                   w_branch[l].astype(BF16), w_out[l].astype(BF16))
        kv = _kv_proj(mem, _row(norm_mem), xa_wkv[l].astype(BF16))
        x = _xattn(x, kv, _row(norm_xattn[l]), xa_wq[l].astype(BF16), xa_wo[l].astype(BF16))
        x = _ffn(x, _row(norm_ffn[l]), ffn_w_up[l].astype(BF16), ffn_conv[l], _row(ffn_conv_b[l]),
                 ffn_w_down[l].astype(BF16), _row(norm_final), l == DEPTH - 1)
    return x
```
